```python
import math
import jax, jax.numpy as jnp
from jax import lax
import numpy as np

D_MODEL = 1024
BATCH = 32
SEQ = 256
DEPTH = 2
DEC_BATCH = 8
DEC_SEQ = 1024
PAST_LEN = 512

GRID_W = 64
N_DIR = 2
NORM_EPS = 1e-6
N_MOD = 6

N_HEADS = 8
N_KV_HEADS = 4
GQA_GROUP = N_HEADS // N_KV_HEADS
HEAD_DIM = 64
ATTN_WIDTH = N_HEADS * HEAD_DIM
KV_WIDTH = N_KV_HEADS * HEAD_DIM
Q_BLOCK = 128
ROPE_BASE = 10000.0
ROPE_PAIRS_PER_AXIS = HEAD_DIM // 4

D_RNN = D_MODEL // 2
RNN_BLOCKS = 8
RNN_BLOCK_W = D_RNN // RNN_BLOCKS
CONV_W = 4
CONV_LEFT = 2
LRU_C = 8.0

D_SSM = D_MODEL // 2
SSM_GROUP_W = 16
SSM_GROUPS = D_SSM // SSM_GROUP_W
SSM_STATE = 64

N_BRANCH = 3

N_EXPERT_GROUPS = 4
EXPERTS_PER_GROUP = 4
N_EXPERTS = N_EXPERT_GROUPS * EXPERTS_PER_GROUP
TOP_K = 2
D_EXPERT = D_MODEL // 4

IN_SPLITS = (ATTN_WIDTH,
             ATTN_WIDTH + KV_WIDTH,
             ATTN_WIDTH + 2 * KV_WIDTH,
             ATTN_WIDTH + 2 * KV_WIDTH + D_RNN,
             ATTN_WIDTH + 2 * KV_WIDTH + 2 * D_RNN,
             ATTN_WIDTH + 2 * KV_WIDTH + 2 * D_RNN + D_SSM)
IN_WIDTH = IN_SPLITS[-1] + N_BRANCH * D_MODEL

kernel_name = 'hybrid_diffusion_prefix_trunk_step'


def rms_norm(x, g):
    x32 = x.astype(jnp.float32)
    y = x32 * lax.rsqrt(jnp.mean(x32 * x32, axis=-1, keepdims=True) + NORM_EPS)
    return (y * g.astype(jnp.float32)).astype(x.dtype)


def adaln_modulation(cond, w_mod, b_mod):
    parts = jnp.split(jax.nn.silu(cond) @ w_mod + b_mod, N_MOD, axis=-1)
    return tuple(p[:, None, :] for p in parts)


def axial_rope(x):
    L = x.shape[1]
    rows = L // GRID_W
    row = jnp.broadcast_to(jnp.arange(rows)[:, None], (rows, GRID_W)).reshape(L).astype(jnp.float32)
    col = jnp.broadcast_to(jnp.arange(GRID_W)[None, :], (rows, GRID_W)).reshape(L).astype(jnp.float32)
    inv = ROPE_BASE ** (-jnp.arange(ROPE_PAIRS_PER_AXIS, dtype=jnp.float32) / ROPE_PAIRS_PER_AXIS)
    ang = jnp.concatenate([row[:, None] * inv, col[:, None] * inv], axis=-1)
    cos = jnp.cos(ang)[None, :, None, :]
    sin = jnp.sin(ang)[None, :, None, :]
    xp = x.astype(jnp.float32).reshape(*x.shape[:-1], HEAD_DIM // 2, 2)
    x1, x2 = xp[..., 0], xp[..., 1]
    out = jnp.stack([x1 * cos - x2 * sin, x1 * sin + x2 * cos], axis=-1).reshape(x.shape)
    return out.astype(x.dtype)


def block_attention(q, k, v):
    Bsz, Lq = q.shape[0], q.shape[1]
    nb = Lq // Q_BLOCK
    qb = q.reshape(Bsz, nb, Q_BLOCK, N_KV_HEADS, GQA_GROUP, HEAD_DIM).transpose(1, 0, 2, 3, 4, 5)
    scale = HEAD_DIM ** -0.5

    def one_block(qi):
        s = jnp.einsum('bqkgd,bskd->bkgqs', qi, k, preferred_element_type=jnp.float32) * scale
        p = jax.nn.softmax(s, axis=-1).astype(v.dtype)
        return jnp.einsum('bkgqs,bskd->bqkgd', p, v)

    o = lax.map(one_block, qb)
    return o.transpose(1, 0, 2, 3, 4, 5).reshape(Bsz, Lq, ATTN_WIDTH)


def linear_scan(a, b, h0, reverse):
    if reverse:
        a, b = jnp.flip(a, 1), jnp.flip(b, 1)
    b = b.at[:, 0].add(a[:, 0] * h0)

    def comb(l, r):
        return (r[0] * l[0], r[0] * l[1] + r[1])

    _, h = lax.associative_scan(comb, (a, b), axis=1)
    final = h[:, -1]
    if reverse:
        h = jnp.flip(h, 1)
    return h, final


def complex_linear_scan(a_re, a_im, b_re, b_im, h0_re, h0_im, reverse):
    if reverse:
        a_re, a_im, b_re, b_im = (jnp.flip(t, 1) for t in (a_re, a_im, b_re, b_im))
    b_re = b_re.at[:, 0].add(a_re[:, 0] * h0_re - a_im[:, 0] * h0_im)
    b_im = b_im.at[:, 0].add(a_re[:, 0] * h0_im + a_im[:, 0] * h0_re)

    def comb(l, r):
        lar, lai, lbr, lbi = l
        rar, rai, rbr, rbi = r
        return (rar * lar - rai * lai, rar * lai + rai * lar,
                rar * lbr - rai * lbi + rbr, rar * lbi + rai * lbr + rbi)

    _, _, h_re, h_im = lax.associative_scan(comb, (a_re, a_im, b_re, b_im), axis=1)
    f_re, f_im = h_re[:, -1], h_im[:, -1]
    if reverse:
        h_re, h_im = jnp.flip(h_re, 1), jnp.flip(h_im, 1)
    return h_re, h_im, f_re, f_im


def centred_depthwise_conv(x, w, b):
    L = x.shape[1]
    xp = jnp.pad(x, ((0, 0), (CONV_LEFT, CONV_W - 1 - CONV_LEFT), (0, 0)))
    out = b
    for j in range(CONV_W):
        out = out + xp[:, j:j + L] * w[j]
    return out


def block_diag_linear(x, w, b):
    xb = x.reshape(*x.shape[:-1], RNN_BLOCKS, RNN_BLOCK_W)
    return jnp.einsum('blni,nij->blnj', xb, w).reshape(x.shape) + b


def rglru_branch(x_in, gate_in, conv_w, conv_b, w_a, b_a, w_i, b_i, lam, w_o, h0):
    xc = centred_depthwise_conv(x_in, conv_w, conv_b)
    xc32 = xc.astype(jnp.float32)
    h_sum = jnp.zeros_like(xc32)
    finals = []
    for d in range(N_DIR):
        r = jax.nn.sigmoid(block_diag_linear(xc, w_a[d], b_a[d]).astype(jnp.float32))
        i = jax.nn.sigmoid(block_diag_linear(xc, w_i[d], b_i[d]).astype(jnp.float32))
        log_a = -LRU_C * r * jax.nn.softplus(-lam[d].astype(jnp.float32))
        a = jnp.exp(log_a)
        u = jnp.sqrt(-jnp.expm1(2.0 * log_a)) * (i * xc32)
        h, hf = linear_scan(a, u, h0[:, d].astype(jnp.float32), reverse=(d == 1))
        h_sum = h_sum + h
        finals.append(hf)
    y = (jax.nn.gelu(gate_in.astype(jnp.float32)) * h_sum).astype(x_in.dtype)
    return y @ w_o, jnp.stack(finals, axis=1)


def s5_branch(u, lam_re, lam_im, log_step, b_re, b_im, c_re, c_im, d_skip, w_glu, s0_re, s0_im):
    Bsz, L, _ = u.shape
    u32 = u.astype(jnp.float32)
    ug = u32.reshape(Bsz, L, SSM_GROUPS, SSM_GROUP_W)
    y = d_skip.astype(jnp.float32) * u32
    fin_re, fin_im = [], []
    for d in range(N_DIR):
        step = jnp.exp(log_step[d].astype(jnp.float32))[:, None]
        lr, li = lam_re[d].astype(jnp.float32), lam_im[d].astype(jnp.float32)
        mag = jnp.exp(lr * step)
        ab_re, ab_im = mag * jnp.cos(li * step), mag * jnp.sin(li * step)
        den = lr * lr + li * li
        nr, ni = ab_re - 1.0, ab_im
        f_re, f_im = (nr * lr + ni * li) / den, (ni * lr - nr * li) / den
        br, bi = b_re[d].astype(jnp.float32), b_im[d].astype(jnp.float32)
        bb_re = f_re[..., None] * br - f_im[..., None] * bi
        bb_im = f_re[..., None] * bi + f_im[..., None] * br
        bu_re = jnp.einsum('gpc,blgc->blgp', bb_re, ug)
        bu_im = jnp.einsum('gpc,blgc->blgp', bb_im, ug)
        a_re = jnp.broadcast_to(ab_re, bu_re.shape)
        a_im = jnp.broadcast_to(ab_im, bu_im.shape)
        x_re, x_im, f_r, f_i = complex_linear_scan(a_re, a_im, bu_re, bu_im,
                                                   s0_re[:, d].astype(jnp.float32), s0_im[:, d].astype(jnp.float32),
                                                   reverse=(d == 1))
        yd = (jnp.einsum('gcp,blgp->blgc', c_re[d].astype(jnp.float32), x_re)
              - jnp.einsum('gcp,blgp->blgc', c_im[d].astype(jnp.float32), x_im))
        y = y + yd.reshape(Bsz, L, D_SSM)
        fin_re.append(f_r)
        fin_im.append(f_i)
    y = jax.nn.gelu(y).astype(u.dtype)
    ya, yb = jnp.split(y @ w_glu, 2, axis=-1)
    return ya * jax.nn.sigmoid(yb), jnp.stack(fin_re, axis=1), jnp.stack(fin_im, axis=1)


def parallel_mixer(h, lp, ctx):
    Bsz, L, _ = h.shape
    z = h @ lp['w_in']
    q, k, v, rx, rg, su, gl = jnp.split(z, IN_SPLITS, axis=-1)
    q = rms_norm(q.reshape(Bsz, L, N_HEADS, HEAD_DIM), lp['q_norm'])
    k = rms_norm(k.reshape(Bsz, L, N_KV_HEADS, HEAD_DIM), lp['k_norm'])
    v = v.reshape(Bsz, L, N_KV_HEADS, HEAD_DIM)
    if ctx is None:
        k_all, v_all = k, v
        h0 = jnp.zeros((Bsz, N_DIR, D_RNN), jnp.float32)
        s0_re = jnp.zeros((Bsz, N_DIR, SSM_GROUPS, SSM_STATE), jnp.float32)
        s0_im = s0_re
    else:
        ctx_k, ctx_v, h0, s0_re, s0_im = ctx
        q = axial_rope(q)
        k_all = jnp.concatenate([ctx_k.astype(k.dtype), axial_rope(k)], axis=1)
        v_all = jnp.concatenate([ctx_v.astype(v.dtype), v], axis=1)
    y_attn = block_attention(q, k_all, v_all) @ lp['w_attn_o']
    y_rnn, rnn_fin = rglru_branch(rx, rg, lp['conv_w'], lp['conv_b'], lp['lru_w_a'], lp['lru_b_a'],
                                  lp['lru_w_i'], lp['lru_b_i'], lp['lru_lambda'], lp['w_rnn_o'], h0)
    y_ssm, ssm_fin_re, ssm_fin_im = s5_branch(su, lp['ssm_lam_re'], lp['ssm_lam_im'], lp['ssm_log_step'],
                                              lp['ssm_b_re'], lp['ssm_b_im'], lp['ssm_c_re'], lp['ssm_c_im'],
                                              lp['ssm_d'], lp['w_glu'], s0_re, s0_im)
    g_a, g_r, g_s = jnp.split(jax.nn.sigmoid(gl), N_BRANCH, axis=-1)
    out = (g_a * y_attn + g_r * y_rnn + g_s * y_ssm) @ lp['w_out']
    return out, (k, v, rnn_fin, ssm_fin_re, ssm_fin_im)


def hier_moe(h, rg_w, rg_b, re_w, re_b, w_gate, w_up, w_down):
    shp = h.shape
    t = h.reshape(-1, D_MODEL)
    g_prob = jax.nn.softmax((t @ rg_w + rg_b).astype(jnp.float32), axis=-1)
    g_sel = jnp.argmax(g_prob, axis=-1)
    g_w = jnp.max(g_prob, axis=-1, keepdims=True)
    e_logits = (t @ re_w + re_b).astype(jnp.float32).reshape(-1, N_EXPERT_GROUPS, EXPERTS_PER_GROUP)
    e_in = jnp.einsum('nge,ng->ne', e_logits, jax.nn.one_hot(g_sel, N_EXPERT_GROUPS, dtype=jnp.float32))
    top_v, top_i = lax.top_k(e_in, TOP_K)
    w_sel = g_w * jax.nn.softmax(top_v, axis=-1)
    expert_id = g_sel[:, None] * EXPERTS_PER_GROUP + top_i
    combine = jnp.einsum('nke,nk->ne', jax.nn.one_hot(expert_id, N_EXPERTS, dtype=jnp.float32), w_sel)
    a = jax.nn.silu(jnp.einsum('nd,edf->nef', t, w_gate)) * jnp.einsum('nd,edf->nef', t, w_up)
    a = a * combine[:, :, None].astype(a.dtype)
    return jnp.einsum('nef,efd->nd', a, w_down).reshape(shp)


def trunk_layer(x, cond, lp, ctx):
    sh1, sc1, g1, sh2, sc2, g2 = adaln_modulation(cond, lp['w_mod'], lp['b_mod'])
    h = rms_norm(x, lp['norm1']) * (1.0 + sc1) + sh1
    mix, ctx_out = parallel_mixer(h, lp, ctx)
    x = x + g1 * mix
    h = rms_norm(x, lp['norm2']) * (1.0 + sc2) + sh2
    x = x + g2 * hier_moe(h, lp['router_g_w'], lp['router_g_b'], lp['router_e_w'], lp['router_e_b'],
                          lp['w_e_gate'], lp['w_e_up'], lp['w_e_down'])
    return x, ctx_out


def setup_inputs(seed: int = 0) -> dict:
    key = jax.random.key(seed)
    ks = iter(jax.random.split(key, 64))
    f32 = jnp.float32

    def nrm(shape, scale):
        return scale * jax.random.normal(next(ks), shape, f32)

    def gain(shape):
        return 1.0 + 0.05 * jax.random.normal(next(ks), shape, f32)

    inp = {}
    inp['x_prompt'] = nrm((BATCH, SEQ, D_MODEL), 1.0)
    inp['x_sample'] = nrm((DEC_BATCH, DEC_SEQ, D_MODEL), 1.0)
    inp['cache_k'] = nrm((DEC_BATCH, DEPTH, PAST_LEN, N_KV_HEADS, HEAD_DIM), 1.0)
    inp['cache_v'] = nrm((DEC_BATCH, DEPTH, PAST_LEN, N_KV_HEADS, HEAD_DIM), 1.0)
    inp['state_rglru'] = nrm((DEC_BATCH, DEPTH, N_DIR, D_RNN), 0.5)
    inp['state_ssm_re'] = nrm((DEC_BATCH, DEPTH, N_DIR, SSM_GROUPS, SSM_STATE), 0.3)
    inp['state_ssm_im'] = nrm((DEC_BATCH, DEPTH, N_DIR, SSM_GROUPS, SSM_STATE), 0.3)
    inp['c'] = nrm((DEC_BATCH, D_MODEL), 1.0)
    inp['c_ctx'] = nrm((D_MODEL,), 1.0)
    inp['w_mod'] = nrm((DEPTH, D_MODEL, N_MOD * D_MODEL), 0.5 * D_MODEL ** -0.5)
    inp['b_mod'] = nrm((DEPTH, N_MOD * D_MODEL), 0.02)
    inp['norm1'] = gain((DEPTH, D_MODEL))
    inp['norm2'] = gain((DEPTH, D_MODEL))
    inp['w_in'] = nrm((DEPTH, D_MODEL, IN_WIDTH), D_MODEL ** -0.5)
    inp['q_norm'] = gain((DEPTH, HEAD_DIM))
    inp['k_norm'] = gain((DEPTH, HEAD_DIM))
    inp['w_attn_o'] = nrm((DEPTH, ATTN_WIDTH, D_MODEL), ATTN_WIDTH ** -0.5)
    inp['conv_w'] = nrm((DEPTH, CONV_W, D_RNN), CONV_W ** -0.5)
    inp['conv_b'] = nrm((DEPTH, D_RNN), 0.02)
    inp['lru_w_a'] = nrm((DEPTH, N_DIR, RNN_BLOCKS, RNN_BLOCK_W, RNN_BLOCK_W), RNN_BLOCK_W ** -0.5)
    inp['lru_b_a'] = nrm((DEPTH, N_DIR, D_RNN), 0.02)
    inp['lru_w_i'] = nrm((DEPTH, N_DIR, RNN_BLOCKS, RNN_BLOCK_W, RNN_BLOCK_W), RNN_BLOCK_W ** -0.5)
    inp['lru_b_i'] = nrm((DEPTH, N_DIR, D_RNN), 0.02)
    a0 = jax.random.uniform(next(ks), (DEPTH, N_DIR, D_RNN), f32,
                            0.9 ** (1.0 / LRU_C), 0.999 ** (1.0 / LRU_C))
    inp['lru_lambda'] = jnp.log(a0) - jnp.log1p(-a0)
    inp['w_rnn_o'] = nrm((DEPTH, D_RNN, D_MODEL), D_RNN ** -0.5)
    inp['ssm_lam_re'] = -0.5 + nrm((DEPTH, N_DIR, SSM_GROUPS, SSM_STATE), 0.01)
    inp['ssm_lam_im'] = jnp.broadcast_to(math.pi * jnp.arange(SSM_STATE, dtype=f32),
                                         (DEPTH, N_DIR, SSM_GROUPS, SSM_STATE))
    inp['ssm_log_step'] = jax.random.uniform(next(ks), (DEPTH, N_DIR, SSM_GROUPS), f32,
                                             math.log(1e-3), math.log(1e-1))
    inp['ssm_b_re'] = nrm((DEPTH, N_DIR, SSM_GROUPS, SSM_STATE, SSM_GROUP_W), (2 * SSM_GROUP_W) ** -0.5)
    inp['ssm_b_im'] = nrm((DEPTH, N_DIR, SSM_GROUPS, SSM_STATE, SSM_GROUP_W), (2 * SSM_GROUP_W) ** -0.5)
    inp['ssm_c_re'] = nrm((DEPTH, N_DIR, SSM_GROUPS, SSM_GROUP_W, SSM_STATE), SSM_STATE ** -0.5)
    inp['ssm_c_im'] = nrm((DEPTH, N_DIR, SSM_GROUPS, SSM_GROUP_W, SSM_STATE), SSM_STATE ** -0.5)
    inp['ssm_d'] = nrm((DEPTH, D_SSM), 1.0)
    inp['w_glu'] = nrm((DEPTH, D_SSM, 2 * D_MODEL), D_SSM ** -0.5)
    inp['w_out'] = nrm((DEPTH, D_MODEL, D_MODEL), D_MODEL ** -0.5)
    inp['router_g_w'] = nrm((DEPTH, D_MODEL, N_EXPERT_GROUPS), D_MODEL ** -0.5)
    inp['router_g_b'] = nrm((DEPTH, N_EXPERT_GROUPS), 0.01)
    inp['router_e_w'] = nrm((DEPTH, D_MODEL, N_EXPERTS), D_MODEL ** -0.5)
    inp['router_e_b'] = nrm((DEPTH, N_EXPERTS), 0.01)
    inp['w_e_gate'] = nrm((DEPTH, N_EXPERTS, D_MODEL, D_EXPERT), D_MODEL ** -0.5)
    inp['w_e_up'] = nrm((DEPTH, N_EXPERTS, D_MODEL, D_EXPERT), D_MODEL ** -0.5)
    inp['w_e_down'] = nrm((DEPTH, N_EXPERTS, D_EXPERT, D_MODEL), D_EXPERT ** -0.5)
    inp['final_norm'] = gain((D_MODEL,))
    return inp


def reference(x_prompt, x_sample, cache_k, cache_v, state_rglru, state_ssm_re, state_ssm_im, c, c_ctx,
              w_mod, b_mod, norm1, norm2, w_in, q_norm, k_norm, w_attn_o, conv_w, conv_b,
              lru_w_a, lru_b_a, lru_w_i, lru_b_i, lru_lambda, w_rnn_o,
              ssm_lam_re, ssm_lam_im, ssm_log_step, ssm_b_re, ssm_b_im, ssm_c_re, ssm_c_im, ssm_d, w_glu,
              w_out, router_g_w, router_g_b, router_e_w, router_e_b, w_e_gate, w_e_up, w_e_down, final_norm):
    x_ctx = x_prompt
    x_lat = x_sample
    ctx_cond = c_ctx[None, :]
    ks, vs, rs, sre, sim = [], [], [], [], []
    for l in range(DEPTH):
        lp = {'w_mod': w_mod[l], 'b_mod': b_mod[l], 'norm1': norm1[l], 'norm2': norm2[l],
              'w_in': w_in[l], 'q_norm': q_norm[l], 'k_norm': k_norm[l], 'w_attn_o': w_attn_o[l],
              'conv_w': conv_w[l], 'conv_b': conv_b[l], 'lru_w_a': lru_w_a[l], 'lru_b_a': lru_b_a[l],
              'lru_w_i': lru_w_i[l], 'lru_b_i': lru_b_i[l], 'lru_lambda': lru_lambda[l], 'w_rnn_o': w_rnn_o[l],
              'ssm_lam_re': ssm_lam_re[l], 'ssm_lam_im': ssm_lam_im[l], 'ssm_log_step': ssm_log_step[l],
              'ssm_b_re': ssm_b_re[l], 'ssm_b_im': ssm_b_im[l], 'ssm_c_re': ssm_c_re[l], 'ssm_c_im': ssm_c_im[l],
              'ssm_d': ssm_d[l], 'w_glu': w_glu[l], 'w_out': w_out[l],
              'router_g_w': router_g_w[l], 'router_g_b': router_g_b[l],
              'router_e_w': router_e_w[l], 'router_e_b': router_e_b[l],
              'w_e_gate': w_e_gate[l], 'w_e_up': w_e_up[l], 'w_e_down': w_e_down[l]}
        x_ctx, (k_l, v_l, r_l, sre_l, sim_l) = trunk_layer(x_ctx, ctx_cond, lp, None)
        ks.append(k_l)
        vs.append(v_l)
        rs.append(r_l)
        sre.append(sre_l)
        sim.append(sim_l)
        ctx = (cache_k[:, l], cache_v[:, l], state_rglru[:, l], state_ssm_re[:, l], state_ssm_im[:, l])
        x_lat, _ = trunk_layer(x_lat, c, lp, ctx)
    y_prompt = rms_norm(x_ctx, final_norm)
    y_sample = rms_norm(x_lat, final_norm)
    new_cache_k = jnp.stack(ks, axis=1)
    new_cache_v = jnp.stack(vs, axis=1)
    new_state_rglru = jnp.stack(rs, axis=1)
    new_state_ssm_re = jnp.stack(sre, axis=1)
    new_state_ssm_im = jnp.stack(sim, axis=1)
    return (y_prompt, y_sample, new_cache_k, new_cache_v, new_state_rglru, new_state_ssm_re, new_state_ssm_im)
```

```python
import functools
import math

import numpy as np
import jax
import jax.numpy as jnp
from jax import lax
from jax.experimental import pallas as pl
from jax.experimental.pallas import tpu as pltpu

F32 = jnp.float32
BF16 = jnp.bfloat16

D_MODEL = 1024
DEPTH = 2
GRID_W = 64
N_DIR = 2
NORM_EPS = 1e-6
N_MOD = 6

N_HEADS = 8
N_KV_HEADS = 4
HEAD_DIM = 64
ATTN_WIDTH = N_HEADS * HEAD_DIM
KV_WIDTH = N_KV_HEADS * HEAD_DIM
ROPE_BASE = 10000.0
ROPE_PAIRS_PER_AXIS = HEAD_DIM // 4

D_RNN = D_MODEL // 2
RNN_BLOCKS = 8
RNN_BLOCK_W = D_RNN // RNN_BLOCKS
CONV_W = 4
CONV_LEFT = 2
LRU_C = 8.0

D_SSM = D_MODEL // 2
SSM_GROUP_W = 16
SSM_GROUPS = D_SSM // SSM_GROUP_W
SSM_STATE = 64

N_EXPERT_GROUPS = 4
EXPERTS_PER_GROUP = 4
N_EXPERTS = N_EXPERT_GROUPS * EXPERTS_PER_GROUP
D_EXPERT = D_MODEL // 4

LANES = 128
SUBLANES = 8
VMEM_LIMIT = 56 * 1024 * 1024

PROJ_W = ATTN_WIDTH + 2 * KV_WIDTH + 2 * D_RNN + D_SSM
GATE_OFF = PROJ_W

SG_GROUPS = LANES // SSM_GROUP_W
SSM_SG = SSM_GROUPS // SG_GROUPS
SG_STATE = SG_GROUPS * SSM_STATE
SSM_FLAT = SSM_SG * 2 * SG_STATE

RNN_CB = LANES
ROUTE_W = LANES


def _cparams(sem, vmem=VMEM_LIMIT):
    return pltpu.CompilerParams(dimension_semantics=sem, vmem_limit_bytes=vmem)


def _dot(a, b):
    return jnp.dot(a, b, preferred_element_type=F32)


def _dot_nt(a, b):
    return lax.dot_general(a, b, (((1,), (1,)), ((), ())), preferred_element_type=F32)


def _split_bf16(a):
    hi = a.astype(BF16)
    lo = (a - hi.astype(F32)).astype(BF16)
    return hi, lo


def _dot3(a, b):
    a_hi, a_lo = _split_bf16(a)
    b_hi, b_lo = _split_bf16(b)
    return _dot(a_hi, b_hi) + (_dot(a_lo, b_hi) + _dot(a_hi, b_lo))


def _sigmoid(x):
    return 1.0 / (1.0 + jnp.exp(-x))


def _rms(x, g):
    ms = jnp.mean(x * x, axis=-1, keepdims=True)
    return x * lax.rsqrt(ms + NORM_EPS) * g


def _modulate(x, g, scale, shift):
    rows, d = x.shape
    y = _rms(x, g).reshape(rows // SUBLANES, SUBLANES, d) * (1.0 + scale)[None] + shift[None]
    return y.reshape(rows, d)


def _per_row(v, m):
    rows, d = v.shape
    return (v.reshape(rows // SUBLANES, SUBLANES, d) * m[None]).reshape(rows, d)


def _mod_kernel(c_ref, w_ref, b_ref, o_ref):
    c = c_ref[...]
    a = c * _sigmoid(c)
    o_ref[...] = _dot3(a, w_ref[...]) + b_ref[...]


def _modulation(cond, w_mod, b_mod):
    rows = cond.shape[0]
    tn = 1536
    width = N_MOD * D_MODEL
    return pl.pallas_call(
        _mod_kernel,
        grid=(DEPTH, width // tn),
        in_specs=[
            pl.BlockSpec((rows, D_MODEL), lambda l, j: (0, 0)),
            pl.BlockSpec((None, D_MODEL, tn), lambda l, j: (l, 0, j)),
            pl.BlockSpec((None, 1, tn), lambda l, j: (l, 0, j)),
        ],
        out_specs=pl.BlockSpec((None, rows, tn), lambda l, j: (l, 0, j)),
        out_shape=jax.ShapeDtypeStruct((DEPTH, rows, width), F32),
        compiler_params=_cparams(("arbitrary", "arbitrary")),
        name="adaln_modulation",
    )(cond, w_mod, b_mod.reshape(DEPTH, 1, width))


def _head_norm(z, s_ref, gain):
    z2 = z * z
    hi, lo = _split_bf16(z2)
    ms = _dot(hi, s_ref[...]) + _dot(lo, s_ref[...])
    return z * lax.rsqrt(ms + NORM_EPS) * gain


def _rope_store(z, cos, sin, out_ref, scale):
    rows = z.shape[0]
    lane = lax.broadcasted_iota(jnp.int32, (rows, LANES), 1)
    even = (lane & 1) == 0
    for c in range(z.shape[1] // LANES):
        zc = z[:, c * LANES:(c + 1) * LANES]
        partner = jnp.where(even, pltpu.roll(zc, LANES - 1, 1), pltpu.roll(zc, 1, 1))
        out_ref[:, c * LANES:(c + 1) * LANES] = (zc * cos + partner * sin) * scale


def _in_proj_kernel(x_ref, mod_ref, g_ref, w_ref, qg_ref, kg_ref, sq_ref, sk_ref, *rest, rope):
    if rope:
        cos_ref, sin_ref, q_ref, k_ref, v_ref, rx_ref, rg_ref, su_ref = rest
    else:
        q_ref, k_ref, v_ref, rx_ref, rg_ref, su_ref = rest
    h = _modulate(x_ref[...], g_ref[...], mod_ref[1], mod_ref[0]).astype(BF16)

    def proj(lo, hi):
        return _dot(h, w_ref[:, lo:hi])

    o = 0
    q = _head_norm(proj(o, o + ATTN_WIDTH), sq_ref, qg_ref[...])
    o += ATTN_WIDTH
    k = _head_norm(proj(o, o + KV_WIDTH), sk_ref, kg_ref[...])
    o += KV_WIDTH
    v_ref[...] = proj(o, o + KV_WIDTH)
    o += KV_WIDTH
    rx_ref[...] = proj(o, o + D_RNN)
    o += D_RNN
    rg_ref[...] = proj(o, o + D_RNN)
    o += D_RNN
    su_ref[...] = proj(o, o + D_SSM)
    qscale = HEAD_DIM ** -0.5
    if rope:
        cos = cos_ref[...]
        sin = sin_ref[...]
        _rope_store(q, cos, sin, q_ref, qscale)
        _rope_store(k, cos, sin, k_ref, 1.0)
    else:
        q_ref[...] = q * qscale
        k_ref[...] = k


def _in_proj(x, mod, g, w_proj, qg, kg, s_q, s_k, rope_tabs, tm=512):
    n = x.shape[0]
    rope = rope_tabs is not None
    full = lambda *shape: pl.BlockSpec(shape, lambda i: (0,) * len(shape))
    rows = lambda w: pl.BlockSpec((tm, w), lambda i: (i, 0))
    in_specs = [rows(D_MODEL), full(N_MOD, SUBLANES, D_MODEL), full(1, D_MODEL), full(D_MODEL, PROJ_W),
                full(1, ATTN_WIDTH), full(1, KV_WIDTH), full(ATTN_WIDTH, ATTN_WIDTH), full(KV_WIDTH, KV_WIDTH)]
    args = [x, mod, g, w_proj, qg, kg, s_q, s_k]
    if rope:
        in_specs += [rows(LANES), rows(LANES)]
        args += list(rope_tabs)
    widths = (ATTN_WIDTH, KV_WIDTH, KV_WIDTH, D_RNN, D_RNN, D_SSM)
    return pl.pallas_call(
        functools.partial(_in_proj_kernel, rope=rope),
        grid=(n // tm,),
        in_specs=in_specs,
        out_specs=[rows(w) for w in widths],
        out_shape=[jax.ShapeDtypeStruct((n, w), F32) for w in widths],
        compiler_params=_cparams(("arbitrary",)),
        name="mixer_in_proj",
    )(*args)


def _attn_kernel(q_ref, k_ref, v_ref, *rest, n_ctx):
    if n_ctx:
        kc_ref, vc_ref, o_ref, kd_ref, va_ref, vb_ref = rest
    else:
        o_ref, kd_ref, va_ref, vb_ref = rest
    lk = k_ref.shape[0]
    tq = q_ref.shape[0]

    def lane_lt(rows):
        return lax.broadcasted_iota(jnp.int32, (rows, LANES), 1) < HEAD_DIM

    @pl.when(pl.program_id(1) == 0)
    def _():
        def fill(src_k, src_v, base, rows):
            low = lane_lt(rows)
            for pair in range(N_KV_HEADS // 2):
                sl = slice(pair * LANES, (pair + 1) * LANES)
                k2 = src_k[:, sl]
                v2 = src_v[:, sl]
                k2r = pltpu.roll(k2, HEAD_DIM, 1)
                v2r = pltpu.roll(v2, HEAD_DIM, 1)
                for odd in range(2):
                    j = 2 * pair + odd
                    kd = jnp.where(low, k2r, k2) if odd else jnp.where(low, k2, k2r)
                    vd = jnp.where(low, v2r, v2) if odd else jnp.where(low, v2, v2r)
                    kd_ref[j, base:base + rows, :] = kd.astype(BF16)
                    va_ref[j, base:base + rows, :] = jnp.where(low, vd, 0.0).astype(BF16)
                    vb_ref[j, base:base + rows, :] = jnp.where(low, 0.0, vd).astype(BF16)

        if n_ctx:
            fill(kc_ref, vc_ref, 0, n_ctx)
        fill(k_ref, v_ref, n_ctx, lk)

    low = lane_lt(tq)
    for j in range(N_KV_HEADS):
        q2 = q_ref[:, j * LANES:(j + 1) * LANES]
        kd = kd_ref[j]
        s0 = _dot_nt(jnp.where(low, q2, 0.0).astype(BF16), kd)
        s1 = _dot_nt(jnp.where(low, 0.0, q2).astype(BF16), kd)
        p0 = jnp.exp(s0 - jnp.max(s0, axis=-1, keepdims=True))
        p1 = jnp.exp(s1 - jnp.max(s1, axis=-1, keepdims=True))
        r0 = 1.0 / jnp.sum(p0, axis=-1, keepdims=True)
        r1 = 1.0 / jnp.sum(p1, axis=-1, keepdims=True)
        o2 = _dot(p0.astype(BF16), va_ref[j]) + _dot(p1.astype(BF16), vb_ref[j])
        o_ref[:, j * LANES:(j + 1) * LANES] = o2 * jnp.where(low, r0, r1)


def _attention(q, k, v, ctx, seq, batch, tq):
    n_ctx = 0 if ctx is None else ctx[0].shape[2]
    tok = lambda w: pl.BlockSpec((None, seq, w), lambda b, i: (b, 0, 0))
    in_specs = [pl.BlockSpec((None, tq, ATTN_WIDTH), lambda b, i: (b, i, 0)), tok(KV_WIDTH), tok(KV_WIDTH)]
    args = [q, k, v]
    if ctx is not None:
        cache_k, cache_v, layer = ctx
        cspec = pl.BlockSpec((None, None, n_ctx, KV_WIDTH), lambda b, i: (b, layer, 0, 0))
        in_specs += [cspec, cspec]
        args += [cache_k, cache_v]
    lt = n_ctx + seq
    return pl.pallas_call(
        functools.partial(_attn_kernel, n_ctx=n_ctx),
        grid=(batch, seq // tq),
        in_specs=in_specs,
        out_specs=pl.BlockSpec((None, tq, ATTN_WIDTH), lambda b, i: (b, i, 0)),
        out_shape=jax.ShapeDtypeStruct((batch, seq, ATTN_WIDTH), F32),
        scratch_shapes=[pltpu.VMEM((N_KV_HEADS, lt, LANES), BF16)] * 3,
        compiler_params=_cparams(("arbitrary", "arbitrary")),
        name="gqa_attention",
    )(*args)


def _rglru_kernel(rx_ref, rg_ref, cw_ref, cb_ref, wg_ref, bg_ref, lam_ref, h0_ref, y_ref, fin_ref,
                  xpad_ref, af_ref, uf_ref, ab_ref, ub_ref, *, seq, batch, chunk):
    n = seq * batch
    pad = CONV_LEFT * batch
    n_chunks = n // chunk
    zeros = jnp.zeros((pad, RNN_CB), F32)
    xpad_ref[0:pad, :] = zeros
    xpad_ref[pad + n:pad + n + pad, :] = zeros

    def copy_in(i, c):
        r0 = pl.multiple_of(i * chunk, chunk)
        xpad_ref[pl.ds(pad + r0, chunk), :] = rx_ref[pl.ds(r0, chunk), :]
        return c

    lax.fori_loop(0, n_chunks, copy_in, 0)

    lam = lam_ref[...]
    neg = -lam
    softplus = jnp.maximum(neg, 0.0) + jnp.log1p(jnp.exp(-jnp.abs(neg)))
    decay = -LRU_C * softplus
    cw = cw_ref[...]
    cb = cb_ref[...]

    def gates(i, c):
        r0 = pl.multiple_of(i * chunk, chunk)
        xc = cb
        for j in range(CONV_W):
            xc = xc + xpad_ref[pl.ds(r0 + j * batch, chunk), :] * cw[j:j + 1, :]
        g = _dot(xc.astype(BF16), wg_ref[...]) + bg_ref[...]
        for d, (a_ref, u_ref) in enumerate(((af_ref, uf_ref), (ab_ref, ub_ref))):
            r = _sigmoid(g[:, (2 * d) * RNN_CB:(2 * d + 1) * RNN_CB])
            gi = _sigmoid(g[:, (2 * d + 1) * RNN_CB:(2 * d + 2) * RNN_CB])
            log_a = r * decay[d:d + 1, :]
            a_ref[pl.ds(r0, chunk), :] = jnp.exp(log_a)
            th = jnp.tanh(log_a)
            u_ref[pl.ds(r0, chunk), :] = jnp.sqrt(-2.0 * th / (1.0 - th)) * (gi * xc)
        return c

    lax.fori_loop(0, n_chunks, gates, 0)

    def step(t, carry):
        hf, hb = carry
        rf = pl.multiple_of(t * batch, batch)
        rb = pl.multiple_of((seq - 1 - t) * batch, batch)
        hf = af_ref[pl.ds(rf, batch), :] * hf + uf_ref[pl.ds(rf, batch), :]
        uf_ref[pl.ds(rf, batch), :] = hf
        hb = ab_ref[pl.ds(rb, batch), :] * hb + ub_ref[pl.ds(rb, batch), :]
        ub_ref[pl.ds(rb, batch), :] = hb
        return hf, hb

    hf, hb = lax.fori_loop(0, seq, step, (h0_ref[0], h0_ref[1]), unroll=8)
    fin_ref[0] = hf
    fin_ref[1] = hb

    def emit(i, c):
        r0 = pl.multiple_of(i * chunk, chunk)
        gate = jax.nn.gelu(rg_ref[pl.ds(r0, chunk), :])
        y_ref[pl.ds(r0, chunk), :] = gate * (uf_ref[pl.ds(r0, chunk), :] + ub_ref[pl.ds(r0, chunk), :])
        return c

    lax.fori_loop(0, n_chunks, emit, 0)


def _rglru(rx, rg, conv_w, conv_b, wg, bg, lam, h0, seq, batch, chunk=512):
    n = seq * batch
    nb = D_RNN // RNN_CB
    col = lambda rows: pl.BlockSpec((rows, RNN_CB), lambda c: (0, c))
    state = pl.BlockSpec((N_DIR, batch, RNN_CB), lambda c: (0, 0, c))
    return pl.pallas_call(
        functools.partial(_rglru_kernel, seq=seq, batch=batch, chunk=chunk),
        grid=(nb,),
        in_specs=[col(n), col(n), col(CONV_W), col(1),
                  pl.BlockSpec((None, RNN_CB, 4 * RNN_CB), lambda c: (c, 0, 0)),
                  pl.BlockSpec((None, 1, 4 * RNN_CB), lambda c: (c, 0, 0)),
                  col(N_DIR), state],
        out_specs=[col(n), state],
        out_shape=[jax.ShapeDtypeStruct((n, D_RNN), F32), jax.ShapeDtypeStruct((N_DIR, batch, D_RNN), F32)],
        scratch_shapes=[pltpu.VMEM((n + 2 * CONV_LEFT * batch, RNN_CB), F32)] + [pltpu.VMEM((n, RNN_CB), F32)] * 4,
        compiler_params=_cparams(("arbitrary",)),
        name="rglru_scan",
    )(rx, rg, conv_w, conv_b, wg, bg, lam, h0)


def _s5_param_kernel(lr_ref, li_ref, ls_ref, br_ref, bi_ref, are_ref, aim_ref, bbr_ref, bbi_ref):
    lr = lr_ref[...]
    li = li_ref[...]
    step = jnp.exp(ls_ref[...])
    mag = jnp.exp(lr * step)
    ab_re = mag * jnp.cos(li * step)
    ab_im = mag * jnp.sin(li * step)
    den = lr * lr + li * li
    nr = ab_re - 1.0
    ni = ab_im
    f_re = (nr * lr + ni * li) / den
    f_im = (ni * lr - nr * li) / den
    are_ref[...] = ab_re
    aim_ref[...] = ab_im
    br = br_ref[...]
    bi = bi_ref[...]
    bbr_ref[...] = f_re[:, None, :] * br - f_im[:, None, :] * bi
    bbi_ref[...] = f_re[:, None, :] * bi + f_im[:, None, :] * br


def _s5_params(lam_re, lam_im, log_step, b_re, b_im):
    rows = DEPTH * N_DIR * SSM_GROUPS
    flat = lambda a: a.reshape(rows, SSM_STATE)
    bt = lambda a: a.reshape(rows, SSM_STATE, SSM_GROUP_W).transpose(0, 2, 1)
    vec = jax.ShapeDtypeStruct((rows, SSM_STATE), F32)
    mat = jax.ShapeDtypeStruct((rows, SSM_GROUP_W, SSM_STATE), F32)
    return pl.pallas_call(
        _s5_param_kernel,
        out_shape=[vec, vec, mat, mat],
        name="s5_discretise",
    )(flat(lam_re), flat(lam_im), log_step.reshape(rows, 1), bt(b_re), bt(b_im))


def _s5_kernel(uf_ref, ub_ref, bm_ref, cm_ref, a_ref, s0_ref, yf_ref, yb_ref, fin_ref, x_ref, st_ref,
               *, batch, pairs_per_pass):
    rows = uf_ref.shape[0]
    steps = rows // batch
    i = pl.program_id(0)

    @pl.when(i == 0)
    def _():
        st_ref[...] = s0_ref[...]

    for d, u_ref in enumerate((uf_ref, ub_ref)):
        u = u_ref[...].astype(BF16)
        for s in range(SSM_SG):
            x_ref[d, :, s * 2 * SG_STATE:(s + 1) * 2 * SG_STATE] = _dot(u[:, s * LANES:(s + 1) * LANES], bm_ref[d, s])

    pair_cols = [(s * 2 * SG_STATE + j * LANES, s * 2 * SG_STATE + SG_STATE + j * LANES)
                 for s in range(SSM_SG) for j in range(SG_STATE // LANES)]
    for p0 in range(0, len(pair_cols), pairs_per_pass):
        cols = pair_cols[p0:p0 + pairs_per_pass]

        def load_state(d):
            return tuple((st_ref[d, :, cr:cr + LANES], st_ref[d, :, ci:ci + LANES]) for cr, ci in cols)

        def step(t, carry):
            new = []
            for d in range(N_DIR):
                tt = t if d == 0 else steps - 1 - t
                r0 = pl.multiple_of(tt * batch, batch)
                new_d = []
                for (cr, ci), (xr, xi) in zip(cols, carry[d]):
                    ar = a_ref[d, :, cr:cr + LANES]
                    ai = a_ref[d, :, ci:ci + LANES]
                    br = x_ref[d, pl.ds(r0, batch), cr:cr + LANES]
                    bi = x_ref[d, pl.ds(r0, batch), ci:ci + LANES]
                    nr = ar * xr - ai * xi + br
                    ni = ar * xi + ai * xr + bi
                    x_ref[d, pl.ds(r0, batch), cr:cr + LANES] = nr
                    x_ref[d, pl.ds(r0, batch), ci:ci + LANES] = ni
                    new_d.append((nr, ni))
                new.append(tuple(new_d))
            return tuple(new)

        final = lax.fori_loop(0, steps, step, (load_state(0), load_state(1)), unroll=2)
        for d in range(N_DIR):
            for (cr, ci), (xr, xi) in zip(cols, final[d]):
                st_ref[d, :, cr:cr + LANES] = xr
                st_ref[d, :, ci:ci + LANES] = xi

    for d, y_ref in enumerate((yf_ref, yb_ref)):
        for s in range(SSM_SG):
            xs = x_ref[d, :, s * 2 * SG_STATE:(s + 1) * 2 * SG_STATE].astype(BF16)
            y_ref[:, s * LANES:(s + 1) * LANES] = _dot(xs, cm_ref[d, s])

    @pl.when(i == pl.num_programs(0) - 1)
    def _():
        fin_ref[...] = st_ref[...]


def _s5(su, bm, cm, a_b, s0, batch, rows=512):
    n = su.shape[0]
    nt = n // rows
    ab = a_b.shape[1]
    pairs_per_pass = max(1, (4 * SUBLANES) // batch)
    full = lambda *shape: pl.BlockSpec(shape, lambda i: (0,) * len(shape))
    fwd = pl.BlockSpec((rows, D_SSM), lambda i: (i, 0))
    bwd = pl.BlockSpec((rows, D_SSM), lambda i: (nt - 1 - i, 0))
    st = full(N_DIR, batch, SSM_FLAT)
    return pl.pallas_call(
        functools.partial(_s5_kernel, batch=batch, pairs_per_pass=pairs_per_pass),
        grid=(nt,),
        in_specs=[fwd, bwd, full(N_DIR, SSM_SG, LANES, 2 * SG_STATE), full(N_DIR, SSM_SG, 2 * SG_STATE, LANES),
                  full(N_DIR, ab, SSM_FLAT), st],
        out_specs=[fwd, bwd, st],
        out_shape=[jax.ShapeDtypeStruct((n, D_SSM), F32)] * 2 + [jax.ShapeDtypeStruct((N_DIR, batch, SSM_FLAT), F32)],
        scratch_shapes=[pltpu.VMEM((N_DIR, rows, SSM_FLAT), F32), pltpu.VMEM((N_DIR, batch, SSM_FLAT), F32)],
        compiler_params=_cparams(("arbitrary",)),
        name="s5_scan",
    )(su, su, bm, cm, a_b, s0)


def _route(logits):
    rows = logits.shape[0]
    lane = lax.broadcasted_iota(jnp.int32, (rows, ROUTE_W), 1)
    neg_inf = -jnp.inf
    big = ROUTE_W

    def row_max(v):
        return jnp.max(v, axis=-1, keepdims=True)

    def first_lane(mask):
        return jnp.min(jnp.where(mask, lane, big), axis=-1, keepdims=True)

    in_g = lane < N_EXPERT_GROUPS
    gl = jnp.where(in_g, logits, neg_inf)
    g_max = row_max(gl)
    g_exp = jnp.where(in_g, jnp.exp(gl - g_max), 0.0)
    g_w = 1.0 / jnp.sum(g_exp, axis=-1, keepdims=True)
    g_sel = first_lane(gl == g_max)
    e_lo = N_EXPERT_GROUPS + g_sel * EXPERTS_PER_GROUP
    in_e = (lane >= e_lo) & (lane < e_lo + EXPERTS_PER_GROUP)
    el = jnp.where(in_e, logits, neg_inf)
    v1 = row_max(el)
    i1 = first_lane(el == v1)
    el2 = jnp.where(lane == i1, neg_inf, el)
    v2 = row_max(el2)
    i2 = first_lane(el2 == v2)
    e2 = jnp.exp(v2 - v1)
    w1 = g_w / (1.0 + e2)
    w2 = g_w * e2 / (1.0 + e2)
    return jnp.where(lane == i1, w1, jnp.where(lane == i2, w2, 0.0))


def _merge_kernel(x_ref, o_ref, yr_ref, su_ref, yf_ref, yb_ref, mod_ref, g1_ref, g2_ref, wgl_ref, wa_ref, wr_ref,
                  dsk_ref, wglu_ref, wout_ref, wrt_ref, brt_ref, x1_ref, h2_ref, comb_ref):
    x = x_ref[...]
    h = _modulate(x, g1_ref[...], mod_ref[1], mod_ref[0]).astype(BF16)
    y_attn = _dot(o_ref[...].astype(BF16), wa_ref[...])
    y_rnn = _dot(yr_ref[...].astype(BF16), wr_ref[...])
    ys = jax.nn.gelu(dsk_ref[...] * su_ref[...] + yf_ref[...] + yb_ref[...]).astype(BF16)
    glu = _dot(ys, wglu_ref[...])
    y_ssm = glu[:, :D_MODEL] * _sigmoid(glu[:, D_MODEL:])
    mix = _sigmoid(_dot(h, wgl_ref[:, 0:D_MODEL])) * y_attn
    mix = mix + _sigmoid(_dot(h, wgl_ref[:, D_MODEL:2 * D_MODEL])) * y_rnn
    mix = mix + _sigmoid(_dot(h, wgl_ref[:, 2 * D_MODEL:3 * D_MODEL])) * y_ssm
    x1 = x + _per_row(_dot(mix.astype(BF16), wout_ref[...]), mod_ref[2])
    x1_ref[...] = x1
    h2 = _modulate(x1, g2_ref[...], mod_ref[4], mod_ref[3])
    h2_ref[...] = h2.astype(BF16)
    comb_ref[...] = _route(_dot3(h2, wrt_ref[...]) + brt_ref[...])


def _merge(x, o, yr, su, yf, yb, mod, g1, g2, w_gl, w_attn_o, w_rnn_o, d_skip, w_glu, w_out, w_rt, b_rt, tm=256):
    n = x.shape[0]
    full = lambda *shape: pl.BlockSpec(shape, lambda i: (0,) * len(shape))
    rows = lambda w: pl.BlockSpec((tm, w), lambda i: (i, 0))
    return pl.pallas_call(
        _merge_kernel,
        grid=(n // tm,),
        in_specs=[rows(D_MODEL), rows(ATTN_WIDTH), rows(D_RNN), rows(D_SSM), rows(D_SSM), rows(D_SSM),
                  full(N_MOD, SUBLANES, D_MODEL), full(1, D_MODEL), full(1, D_MODEL),
                  full(D_MODEL, 3 * D_MODEL), full(ATTN_WIDTH, D_MODEL), full(D_RNN, D_MODEL), full(1, D_SSM),
                  full(D_SSM, 2 * D_MODEL), full(D_MODEL, D_MODEL), full(D_MODEL, ROUTE_W), full(1, ROUTE_W)],
        out_specs=[rows(D_MODEL), rows(D_MODEL), rows(ROUTE_W)],
        out_shape=[jax.ShapeDtypeStruct((n, D_MODEL), F32), jax.ShapeDtypeStruct((n, D_MODEL), BF16),
                   jax.ShapeDtypeStruct((n, ROUTE_W), F32)],
        compiler_params=_cparams(("arbitrary",)),
        name="mixer_merge",
    )(x, o, yr, su, yf, yb, mod, g1, g2, w_gl, w_attn_o, w_rnn_o, d_skip, w_glu, w_out, w_rt, b_rt)


def _moe_kernel(x1_ref, h2_ref, comb_ref, mod_ref, wgu_ref, wd_ref, *rest, final):
    if final:
        gf_ref, x2_ref, y_ref, acc_ref = rest
    else:
        x2_ref, acc_ref = rest
    e = pl.program_id(1)

    @pl.when(e == 0)
    def _():
        acc_ref[...] = jnp.zeros_like(acc_ref)

    comb = comb_ref[...]
    lane = lax.broadcasted_iota(jnp.int32, comb.shape, 1)
    w_e = jnp.sum(jnp.where(lane == e + N_EXPERT_GROUPS, comb, 0.0), axis=-1, keepdims=True)
    gu = _dot(h2_ref[...], wgu_ref[...])
    gate = gu[:, :D_EXPERT]
    act = gate * _sigmoid(gate) * gu[:, D_EXPERT:] * w_e
    acc_ref[...] += _dot(act.astype(BF16), wd_ref[...])

    @pl.when(e == N_EXPERTS - 1)
    def _():
        x2 = x1_ref[...] + _per_row(acc_ref[...], mod_ref[5])
        x2_ref[...] = x2
        if final:
            y_ref[...] = _rms(x2, gf_ref[...])


def _moe(x1, h2, comb, mod, w_gu, w_d, g_final, tm=1024):
    n = x1.shape[0]
    final = g_final is not None
    rows = lambda w: pl.BlockSpec((tm, w), lambda i, e: (i, 0))
    in_specs = [rows(D_MODEL), rows(D_MODEL), rows(ROUTE_W),
                pl.BlockSpec((N_MOD, SUBLANES, D_MODEL), lambda i, e: (0, 0, 0)),
                pl.BlockSpec((None, D_MODEL, 2 * D_EXPERT), lambda i, e: (e, 0, 0)),
                pl.BlockSpec((None, D_EXPERT, D_MODEL), lambda i, e: (e, 0, 0))]
    args = [x1, h2, comb, mod, w_gu, w_d]
    out_specs = [rows(D_MODEL)]
    out_shape = [jax.ShapeDtypeStruct((n, D_MODEL), F32)]
    if final:
        in_specs.append(pl.BlockSpec((1, D_MODEL), lambda i, e: (0, 0)))
        args.append(g_final)
        out_specs.append(rows(D_MODEL))
        out_shape.append(jax.ShapeDtypeStruct((n, D_MODEL), F32))
    return pl.pallas_call(
        functools.partial(_moe_kernel, final=final),
        grid=(n // tm, N_EXPERTS),
        in_specs=in_specs,
        out_specs=out_specs,
        out_shape=out_shape,
        scratch_shapes=[pltpu.VMEM((tm, D_MODEL), F32)],
        compiler_params=_cparams(("arbitrary", "arbitrary")),
        name="hier_moe",
    )(*args)


def _block_diag(blocks):
    n, r, c = blocks.shape
    eye = jnp.eye(n, dtype=blocks.dtype)
    return (eye[:, None, :, None] * blocks[:, :, None, :]).reshape(n * r, n * c)


def _head_mean_matrix(width):
    idx = np.arange(width) // HEAD_DIM
    return jnp.asarray((idx[:, None] == idx[None, :]).astype(np.float32) / HEAD_DIM, dtype=BF16)


def _rope_tables(seq, batch):
    t = np.arange(seq)
    row = (t // GRID_W).astype(np.float64)
    col = (t % GRID_W).astype(np.float64)
    inv = ROPE_BASE ** (-np.arange(ROPE_PAIRS_PER_AXIS, dtype=np.float64) / ROPE_PAIRS_PER_AXIS)
    inv = inv.astype(np.float32).astype(np.float64)
    ang = np.concatenate([row[:, None] * inv, col[:, None] * inv], axis=-1)
    ang = ang.astype(np.float32).astype(np.float64)
    cos = np.repeat(np.cos(ang), 2, axis=-1)
    sin = np.repeat(np.sin(ang), 2, axis=-1) * np.tile(np.array([-1.0, 1.0]), HEAD_DIM // 2)
    expand = lambda a: jnp.asarray(np.repeat(np.tile(a, (1, LANES // HEAD_DIM)), batch, axis=0), dtype=F32)
    return expand(cos), expand(sin)


def _layer_params(l, p, s5p):
    lp = {}
    w_in = p['w_in'][l]
    lp['w_proj'] = w_in[:, :PROJ_W].astype(BF16)
    lp['w_gl'] = w_in[:, GATE_OFF:].astype(BF16)
    lp['norm1'] = p['norm1'][l][None]
    lp['norm2'] = p['norm2'][l][None]
    lp['qg'] = jnp.tile(p['q_norm'][l], N_HEADS)[None]
    lp['kg'] = jnp.tile(p['k_norm'][l], N_KV_HEADS)[None]
    lp['w_attn_o'] = p['w_attn_o'][l].astype(BF16)
    lp['w_rnn_o'] = p['w_rnn_o'][l].astype(BF16)
    lp['w_glu'] = p['w_glu'][l].astype(BF16)
    lp['w_out'] = p['w_out'][l].astype(BF16)
    lp['d_skip'] = p['ssm_d'][l][None]
    lp['conv_w'] = p['conv_w'][l]
    lp['conv_b'] = p['conv_b'][l][None]
    lp['lam'] = p['lru_lambda'][l]
    nb = D_RNN // RNN_CB
    per = RNN_CB // RNN_BLOCK_W
    mats, biases = [], []
    for d in range(N_DIR):
        for w, b in ((p['lru_w_a'][l, d], p['lru_b_a'][l, d]), (p['lru_w_i'][l, d], p['lru_b_i'][l, d])):
            mats.append(jnp.stack([_block_diag(w[c * per:(c + 1) * per]) for c in range(nb)]))
            biases.append(b.reshape(nb, 1, RNN_CB))
    lp['wg'] = jnp.concatenate(mats, axis=-1).astype(BF16)
    lp['bg'] = jnp.concatenate(biases, axis=-1)
    ab_re, ab_im, bb_re, bb_im = s5p
    sel = lambda a: a.reshape((DEPTH, N_DIR, SSM_SG, SG_GROUPS) + a.shape[1:])[l]
    bre, bim = sel(bb_re), sel(bb_im)
    bm = [[jnp.concatenate([_block_diag(bre[d, s]), _block_diag(bim[d, s])], axis=1) for s in range(SSM_SG)]
          for d in range(N_DIR)]
    lp['bm'] = jnp.stack([jnp.stack(r) for r in bm]).astype(BF16)
    cre = p['ssm_c_re'][l].reshape(N_DIR, SSM_SG, SG_GROUPS, SSM_GROUP_W, SSM_STATE).transpose(0, 1, 2, 4, 3)
    cim = p['ssm_c_im'][l].reshape(N_DIR, SSM_SG, SG_GROUPS, SSM_GROUP_W, SSM_STATE).transpose(0, 1, 2, 4, 3)
    cm = [[jnp.concatenate([_block_diag(cre[d, s]), -_block_diag(cim[d, s])], axis=0) for s in range(SSM_SG)]
          for d in range(N_DIR)]
    lp['cm'] = jnp.stack([jnp.stack(r) for r in cm]).astype(BF16)
    are, aim = sel(ab_re), sel(ab_im)
    lp['a_flat'] = jnp.concatenate([are.reshape(N_DIR, SSM_SG, SG_STATE), aim.reshape(N_DIR, SSM_SG, SG_STATE)],
                                   axis=-1).reshape(N_DIR, 1, SSM_FLAT)
    w_rt = jnp.zeros((D_MODEL, ROUTE_W), F32)
    w_rt = w_rt.at[:, :N_EXPERT_GROUPS].set(p['router_g_w'][l])
    w_rt = w_rt.at[:, N_EXPERT_GROUPS:N_EXPERT_GROUPS + N_EXPERTS].set(p['router_e_w'][l])
    b_rt = jnp.zeros((1, ROUTE_W), F32)
    b_rt = b_rt.at[0, :N_EXPERT_GROUPS].set(p['router_g_b'][l])
    b_rt = b_rt.at[0, N_EXPERT_GROUPS:N_EXPERT_GROUPS + N_EXPERTS].set(p['router_e_b'][l])
    lp['w_rt'] = w_rt
    lp['b_rt'] = b_rt
    lp['w_gu'] = jnp.concatenate([p['w_e_gate'][l], p['w_e_up'][l]], axis=-1).astype(BF16)
    lp['w_d'] = p['w_e_down'][l].astype(BF16)
    return lp


def _flat_state(re, im):
    b = re.shape[0]
    r = re.transpose(1, 0, 2, 3).reshape(N_DIR, b, SSM_SG, SG_STATE)
    i = im.transpose(1, 0, 2, 3).reshape(N_DIR, b, SSM_SG, SG_STATE)
    return jnp.concatenate([r, i], axis=-1).reshape(N_DIR, b, SSM_FLAT)


def _unflat_state(flat):
    b = flat.shape[1]
    f = flat.reshape(N_DIR, b, SSM_SG, 2, SG_STATE)
    re = f[:, :, :, 0].reshape(N_DIR, b, SSM_GROUPS, SSM_STATE).transpose(1, 0, 2, 3)
    im = f[:, :, :, 1].reshape(N_DIR, b, SSM_GROUPS, SSM_STATE).transpose(1, 0, 2, 3)
    return re, im


def _trunk_layer(x, mod, lp, seq, batch, ctx, rope_tabs, s_q, s_k, g_final, attn_tq):
    q, k, v, rx, rg, su = _in_proj(x, mod, lp['norm1'], lp['w_proj'], lp['qg'], lp['kg'], s_q, s_k, rope_tabs)
    bm3 = lambda a: a.reshape(seq, batch, a.shape[-1]).transpose(1, 0, 2)
    if ctx is None:
        attn_ctx = None
        h0 = jnp.zeros((N_DIR, batch, D_RNN), F32)
        s0 = jnp.zeros((N_DIR, batch, SSM_FLAT), F32)
    else:
        cache_k, cache_v, layer, h0, s0 = ctx
        attn_ctx = (cache_k, cache_v, layer)
    k_bm, v_bm = bm3(k), bm3(v)
    o = _attention(bm3(q), k_bm, v_bm, attn_ctx, seq, batch, attn_tq)
    o = o.transpose(1, 0, 2).reshape(seq * batch, ATTN_WIDTH)
    yr, rnn_fin = _rglru(rx, rg, lp['conv_w'], lp['conv_b'], lp['wg'], lp['bg'], lp['lam'], h0, seq, batch)
    a_b = jnp.broadcast_to(lp['a_flat'], (N_DIR, batch, SSM_FLAT))
    yf, yb, ssm_fin = _s5(su, lp['bm'], lp['cm'], a_b, s0, batch)
    x1, h2, comb = _merge(x, o, yr, su, yf, yb, mod, lp['norm1'], lp['norm2'], lp['w_gl'], lp['w_attn_o'],
                          lp['w_rnn_o'], lp['d_skip'], lp['w_glu'], lp['w_out'], lp['w_rt'], lp['b_rt'])
    outs = _moe(x1, h2, comb, mod, lp['w_gu'], lp['w_d'], g_final)
    return outs, (k_bm, v_bm, rnn_fin, ssm_fin)


def kernel(x_prompt, x_sample, cache_k, cache_v, state_rglru, state_ssm_re, state_ssm_im, c, c_ctx, w_mod, b_mod, norm1, norm2, w_in, q_norm, k_norm, w_attn_o, conv_w, conv_b, lru_w_a, lru_b_a, lru_w_i, lru_b_i, lru_lambda, w_rnn_o, ssm_lam_re, ssm_lam_im, ssm_log_step, ssm_b_re, ssm_b_im, ssm_c_re, ssm_c_im, ssm_d, w_glu, w_out, router_g_w, router_g_b, router_e_w, router_e_b, w_e_gate, w_e_up, w_e_down, final_norm):
    p = dict(norm1=norm1, norm2=norm2, w_in=w_in, q_norm=q_norm, k_norm=k_norm, w_attn_o=w_attn_o, conv_w=conv_w,
             conv_b=conv_b, lru_w_a=lru_w_a, lru_b_a=lru_b_a, lru_w_i=lru_w_i, lru_b_i=lru_b_i,
             lru_lambda=lru_lambda, w_rnn_o=w_rnn_o, ssm_c_re=ssm_c_re, ssm_c_im=ssm_c_im, ssm_d=ssm_d, w_glu=w_glu,
             w_out=w_out, router_g_w=router_g_w, router_g_b=router_g_b, router_e_w=router_e_w,
             router_e_b=router_e_b, w_e_gate=w_e_gate, w_e_up=w_e_up, w_e_down=w_e_down)
    bc, lc, _ = x_prompt.shape
    bd, ld, _ = x_sample.shape
    past = cache_k.shape[2]

    n_cond = 1 + bd
    cond_rows = -(-n_cond // SUBLANES) * SUBLANES
    cond = jnp.zeros((cond_rows, D_MODEL), F32).at[0].set(c_ctx).at[1:n_cond].set(c)
    mods = _modulation(cond, w_mod, b_mod).reshape(DEPTH, cond_rows, N_MOD, D_MODEL)
    s5p = _s5_params(ssm_lam_re, ssm_lam_im, ssm_log_step, ssm_b_re, ssm_b_im)

    s_q = _head_mean_matrix(ATTN_WIDTH)
    s_k = _head_mean_matrix(KV_WIDTH)
    rope_tabs = _rope_tables(ld, bd)
    final_g = final_norm[None]

    x_ctx = x_prompt.transpose(1, 0, 2).reshape(lc * bc, D_MODEL)
    x_lat = x_sample.transpose(1, 0, 2).reshape(ld * bd, D_MODEL)
    ck = cache_k.reshape(bd, DEPTH, past, KV_WIDTH)
    cv = cache_v.reshape(bd, DEPTH, past, KV_WIDTH)
    ks, vs, rs, sre, sim = [], [], [], [], []
    y_ctx = y_lat = None
    for l in range(DEPTH):
        lp = _layer_params(l, p, s5p)
        g_final = final_g if l == DEPTH - 1 else None
        mod_ctx = jnp.broadcast_to(mods[l, 0][:, None, :], (N_MOD, SUBLANES, D_MODEL))
        mod_lat = mods[l, 1:n_cond].transpose(1, 0, 2)
        outs, (k_l, v_l, r_l, s_l) = _trunk_layer(x_ctx, mod_ctx, lp, lc, bc, None, None, s_q, s_k, g_final, lc)
        x_ctx = outs[0]
        if g_final is not None:
            y_ctx = outs[1]
        ks.append(k_l.reshape(bc, lc, N_KV_HEADS, HEAD_DIM))
        vs.append(v_l.reshape(bc, lc, N_KV_HEADS, HEAD_DIM))
        rs.append(r_l.transpose(1, 0, 2))
        s_re, s_im = _unflat_state(s_l)
        sre.append(s_re)
        sim.append(s_im)
        ctx = (ck, cv, l, state_rglru[:, l].transpose(1, 0, 2), _flat_state(state_ssm_re[:, l], state_ssm_im[:, l]))
        outs, _ = _trunk_layer(x_lat, mod_lat, lp, ld, bd, ctx, rope_tabs, s_q, s_k, g_final, 256)
        x_lat = outs[0]
        if g_final is not None:
            y_lat = outs[1]
    y_prompt = y_ctx.reshape(lc, bc, D_MODEL).transpose(1, 0, 2)
    y_sample = y_lat.reshape(ld, bd, D_MODEL).transpose(1, 0, 2)
    return (y_prompt, y_sample, jnp.stack(ks, axis=1), jnp.stack(vs, axis=1), jnp.stack(rs, axis=1),
            jnp.stack(sre, axis=1), jnp.stack(sim, axis=1))
```

```python
import functools

import numpy as np
import jax
import jax.numpy as jnp
from jax import lax
from jax.experimental import pallas as pl
from jax.experimental.pallas import tpu as pltpu

F32 = jnp.float32
BF16 = jnp.bfloat16

D_MODEL = 1024
DEPTH = 2
GRID_W = 64
N_DIR = 2
NORM_EPS = 1e-6
N_MOD = 6

N_HEADS = 8
N_KV_HEADS = 4
HEAD_DIM = 64
ATTN_WIDTH = N_HEADS * HEAD_DIM
KV_WIDTH = N_KV_HEADS * HEAD_DIM
ROPE_BASE = 10000.0
ROPE_PAIRS_PER_AXIS = HEAD_DIM // 4

D_RNN = D_MODEL // 2
RNN_BLOCKS = 8
RNN_BLOCK_W = D_RNN // RNN_BLOCKS
CONV_W = 4
CONV_LEFT = 2
LRU_C = 8.0

D_SSM = D_MODEL // 2
SSM_GROUP_W = 16
SSM_GROUPS = D_SSM // SSM_GROUP_W
SSM_STATE = 64

N_EXPERT_GROUPS = 4
EXPERTS_PER_GROUP = 4
N_EXPERTS = N_EXPERT_GROUPS * EXPERTS_PER_GROUP
D_EXPERT = D_MODEL // 4

LANES = 128
SUBLANES = 8
VMEM_LIMIT = 56 * 1024 * 1024

BG = SUBLANES
PROJ_W = ATTN_WIDTH + 2 * KV_WIDTH + 2 * D_RNN + D_SSM
N_SLAB = D_RNN // LANES

SG_GROUPS = LANES // SSM_GROUP_W
SSM_SG = SSM_GROUPS // SG_GROUPS
SG_STATE = SG_GROUPS * SSM_STATE
SSM_FLAT = SSM_SG * 2 * SG_STATE

ROUTE_W = LANES


def _cparams(sem, vmem=VMEM_LIMIT):
    return pltpu.CompilerParams(dimension_semantics=sem, vmem_limit_bytes=vmem)


def _const(shape, lead=(), single=False):
    idx = tuple(lead) + (0,) * len(shape)
    block = (None,) * len(lead) + tuple(shape)
    if single:
        return pl.BlockSpec(block, lambda *_: idx, pipeline_mode=pl.Buffered(1))
    return pl.BlockSpec(block, lambda *_: idx)


def _dot(a, b):
    return jnp.dot(a, b, preferred_element_type=F32)


def _dot_nt(a, b):
    return lax.dot_general(a, b, (((1,), (1,)), ((), ())), preferred_element_type=F32)


def _split_bf16(a):
    hi = a.astype(BF16)
    lo = (a - hi.astype(F32)).astype(BF16)
    return hi, lo


def _dot3(a, b):
    a_hi, a_lo = _split_bf16(a)
    b_hi, b_lo = _split_bf16(b)
    return _dot(a_hi, b_hi) + (_dot(a_lo, b_hi) + _dot(a_hi, b_lo))


def _sigmoid(x):
    return 0.5 * jnp.tanh(0.5 * x) + 0.5


def _rms(x, g):
    ms = jnp.mean(x * x, axis=-1, keepdims=True)
    return x * lax.rsqrt(ms + NORM_EPS) * g


def _modulate(x, g, mod_ref, b, shift_i, scale_i):
    return _rms(x, g) * (1.0 + mod_ref[b, scale_i:scale_i + 1, :]) + mod_ref[b, shift_i:shift_i + 1, :]


def _strided_rows(b, n):
    return pl.ds(b, n, stride=BG)


def _mod_kernel(c_ref, w_ref, b_ref, o_ref):
    c = c_ref[...]
    a = c * _sigmoid(c)
    o_ref[...] = _dot3(a, w_ref[...]) + b_ref[...]


def _modulation(cond, w_mod, b_mod):
    rows = cond.shape[0]
    tn = 1536
    width = N_MOD * D_MODEL
    return pl.pallas_call(
        _mod_kernel,
        grid=(DEPTH, width // tn),
        in_specs=[
            pl.BlockSpec((rows, D_MODEL), lambda l, j: (0, 0)),
            pl.BlockSpec((None, D_MODEL, tn), lambda l, j: (l, 0, j)),
            pl.BlockSpec((None, 1, tn), lambda l, j: (l, 0, j)),
        ],
        out_specs=pl.BlockSpec((None, rows, tn), lambda l, j: (l, 0, j)),
        out_shape=jax.ShapeDtypeStruct((DEPTH, rows, width), F32),
        compiler_params=_cparams(("arbitrary", "arbitrary")),
        name="adaln_modulation",
    )(cond, w_mod, b_mod.reshape(DEPTH, 1, width))


def _head_norm(z, s_ref, gain):
    ms = _dot((z * z).astype(BF16), s_ref[...])
    return z * lax.rsqrt(ms + NORM_EPS) * gain


def _rope(z, cos, sin):
    rows = z.shape[0]
    tt = cos.shape[0]
    lane = lax.broadcasted_iota(jnp.int32, (rows, LANES), 1)
    partner = jnp.where((lane & 1) == 0, pltpu.roll(z, LANES - 1, 1), pltpu.roll(z, 1, 1))
    z3 = z.reshape(rows // tt, tt, LANES)
    p3 = partner.reshape(rows // tt, tt, LANES)
    return z3 * cos[None] + p3 * sin[None]


def _in_proj_kernel(x_ref, mod_ref, g_ref, w_ref, qg_ref, kg_ref, sq_ref, sk_ref, *rest, rope):
    if rope:
        cos_ref, sin_ref, q_ref, k_ref, v_ref, rx_ref, rg_ref, su_ref, h_scr = rest
    else:
        q_ref, k_ref, v_ref, rx_ref, rg_ref, su_ref, h_scr = rest
    tt = x_ref.shape[1]
    g = g_ref[...]
    for b in range(BG):
        h_scr[b * tt:(b + 1) * tt, :] = _modulate(x_ref[b], g, mod_ref, b, 0, 1).astype(BF16)
    h = h_scr[...]

    def proj(lo, width):
        return _dot(h, w_ref[:, lo:lo + width])

    qscale = HEAD_DIM ** -0.5
    q = _head_norm(proj(0, ATTN_WIDTH), sq_ref, qg_ref[...]) * qscale
    k = _head_norm(proj(ATTN_WIDTH, KV_WIDTH), sk_ref, kg_ref[...])
    if rope:
        cos = cos_ref[...]
        sin = sin_ref[...]
        for dst, z in ((q_ref, q), (k_ref, k)):
            for c in range(z.shape[1] // LANES):
                dst[:, :, c * LANES:(c + 1) * LANES] = _rope(z[:, c * LANES:(c + 1) * LANES], cos, sin)
    else:
        q_ref[...] = q.reshape(BG, tt, ATTN_WIDTH)
        k_ref[...] = k.reshape(BG, tt, KV_WIDTH)
    off = ATTN_WIDTH + KV_WIDTH
    v_ref[...] = proj(off, KV_WIDTH).reshape(BG, tt, KV_WIDTH)
    off += KV_WIDTH
    for dst in (rx_ref, rg_ref, su_ref):
        z = proj(off, D_RNN)
        off += D_RNN
        for b in range(BG):
            for c in range(N_SLAB):
                dst.at[c][_strided_rows(b, tt), :] = z[b * tt:(b + 1) * tt, c * LANES:(c + 1) * LANES]


def _in_proj(x, mod, pp, l, rope_tabs, tt=64):
    batch, seq, _ = x.shape
    nbg = batch // BG
    rope = rope_tabs is not None
    tok = lambda w: pl.BlockSpec((BG, tt, w), lambda g, i: (g, i, 0))
    slab = pl.BlockSpec((None, N_SLAB, tt * BG, LANES), lambda g, i: (g, 0, i, 0))
    in_specs = [tok(D_MODEL), _const((BG, N_MOD, D_MODEL)), _const((1, D_MODEL), (l,)),
                _const((D_MODEL, PROJ_W), (l,), single=True), _const((1, ATTN_WIDTH), (l,)),
                _const((1, KV_WIDTH), (l,)), _const((ATTN_WIDTH, ATTN_WIDTH)), _const((KV_WIDTH, KV_WIDTH))]
    args = [x, mod, pp['norm1'], pp['w_proj'], pp['qg'], pp['kg'], pp['s_q'], pp['s_k']]
    if rope:
        tab = pl.BlockSpec((tt, LANES), lambda g, i: (i, 0))
        in_specs += [tab, tab]
        args += list(rope_tabs)
    tok_shape = lambda w: jax.ShapeDtypeStruct((batch, seq, w), F32)
    slab_shape = jax.ShapeDtypeStruct((nbg, N_SLAB, seq * BG, LANES), F32)
    return pl.pallas_call(
        functools.partial(_in_proj_kernel, rope=rope),
        grid=(nbg, seq // tt),
        in_specs=in_specs,
        out_specs=[tok(ATTN_WIDTH), tok(KV_WIDTH), tok(KV_WIDTH), slab, slab, slab],
        out_shape=[tok_shape(ATTN_WIDTH), tok_shape(KV_WIDTH), tok_shape(KV_WIDTH), slab_shape, slab_shape, slab_shape],
        scratch_shapes=[pltpu.VMEM((BG * tt, D_MODEL), BF16)],
        compiler_params=_cparams(("arbitrary", "arbitrary")),
        name="mixer_in_proj",
    )(*args)


def _attn_kernel(q_ref, k_ref, v_ref, *rest, n_ctx):
    if n_ctx:
        kc_ref, vc_ref, o_ref, kd_ref, va_ref, vb_ref = rest
    else:
        o_ref, kd_ref, va_ref, vb_ref = rest
    lk = k_ref.shape[0]
    tq = q_ref.shape[0]

    def lane_lt(rows):
        return lax.broadcasted_iota(jnp.int32, (rows, LANES), 1) < HEAD_DIM

    @pl.when(pl.program_id(1) == 0)
    def _():
        def fill(src_k, src_v, base, rows):
            low = lane_lt(rows)
            for pair in range(N_KV_HEADS // 2):
                sl = slice(pair * LANES, (pair + 1) * LANES)
                k2 = src_k[:, sl]
                v2 = src_v[:, sl]
                k2r = pltpu.roll(k2, HEAD_DIM, 1)
                v2r = pltpu.roll(v2, HEAD_DIM, 1)
                for odd in range(2):
                    j = 2 * pair + odd
                    kd = jnp.where(low, k2r, k2) if odd else jnp.where(low, k2, k2r)
                    vd = jnp.where(low, v2r, v2) if odd else jnp.where(low, v2, v2r)
                    kd_ref[j, base:base + rows, :] = kd.astype(BF16)
                    va_ref[j, base:base + rows, :] = jnp.where(low, vd, 0.0).astype(BF16)
                    vb_ref[j, base:base + rows, :] = jnp.where(low, 0.0, vd).astype(BF16)

        if n_ctx:
            fill(kc_ref, vc_ref, 0, n_ctx)
        fill(k_ref, v_ref, n_ctx, lk)

    low = lane_lt(tq)
    for j in range(N_KV_HEADS):
        q2 = q_ref[:, j * LANES:(j + 1) * LANES]
        kd = kd_ref[j]
        s0 = _dot_nt(jnp.where(low, q2, 0.0).astype(BF16), kd)
        s1 = _dot_nt(jnp.where(low, 0.0, q2).astype(BF16), kd)
        p0 = jnp.exp(s0 - jnp.max(s0, axis=-1, keepdims=True))
        p1 = jnp.exp(s1 - jnp.max(s1, axis=-1, keepdims=True))
        r0 = 1.0 / jnp.sum(p0, axis=-1, keepdims=True)
        r1 = 1.0 / jnp.sum(p1, axis=-1, keepdims=True)
        o2 = _dot(p0.astype(BF16), va_ref[j]) + _dot(p1.astype(BF16), vb_ref[j])
        o_ref[:, j * LANES:(j + 1) * LANES] = o2 * jnp.where(low, r0, r1)


def _attention(q, k, v, ctx, tq):
    batch, seq, _ = q.shape
    n_ctx = 0 if ctx is None else ctx[0].shape[2]
    tok = lambda w: pl.BlockSpec((None, seq, w), lambda b, i: (b, 0, 0))
    in_specs = [pl.BlockSpec((None, tq, ATTN_WIDTH), lambda b, i: (b, i, 0)), tok(KV_WIDTH), tok(KV_WIDTH)]
    args = [q, k, v]
    if ctx is not None:
        cache_k, cache_v, layer = ctx
        cspec = pl.BlockSpec((None, None, n_ctx, KV_WIDTH), lambda b, i: (b, layer, 0, 0))
        in_specs += [cspec, cspec]
        args += [cache_k, cache_v]
    lt = n_ctx + seq
    return pl.pallas_call(
        functools.partial(_attn_kernel, n_ctx=n_ctx),
        grid=(batch, seq // tq),
        in_specs=in_specs,
        out_specs=pl.BlockSpec((None, tq, ATTN_WIDTH), lambda b, i: (b, i, 0)),
        out_shape=jax.ShapeDtypeStruct((batch, seq, ATTN_WIDTH), F32),
        scratch_shapes=[pltpu.VMEM((N_KV_HEADS, lt, LANES), BF16)] * 3,
        compiler_params=_cparams(("arbitrary", "arbitrary")),
        name="gqa_attention",
    )(*args)


def _rglru_kernel(rx_ref, rg_ref, cw_ref, cb_ref, wg_ref, bg_ref, lam_ref, h0_ref, y_ref, fin_ref,
                  xpad_ref, af_ref, uf_ref, ab_ref, ub_ref, *, seq, chunk):
    n = seq * BG
    pad = CONV_LEFT * BG
    n_chunks = n // chunk
    zeros = jnp.zeros((pad, LANES), F32)
    xpad_ref[0:pad, :] = zeros
    xpad_ref[pad + n:pad + n + pad, :] = zeros

    def copy_in(i, c):
        r0 = pl.multiple_of(i * chunk, chunk)
        xpad_ref[pl.ds(pad + r0, chunk), :] = rx_ref[pl.ds(r0, chunk), :]
        return c

    lax.fori_loop(0, n_chunks, copy_in, 0)

    neg = -lam_ref[...]
    softplus = jnp.maximum(neg, 0.0) + jnp.log1p(jnp.exp(-jnp.abs(neg)))
    decay = -LRU_C * softplus
    cw = cw_ref[...]
    cb = cb_ref[...]

    def gates(i, c):
        r0 = pl.multiple_of(i * chunk, chunk)
        xc = cb
        for j in range(CONV_W):
            xc = xc + xpad_ref[pl.ds(r0 + j * BG, chunk), :] * cw[j:j + 1, :]
        g = _dot(xc.astype(BF16), wg_ref[...]) + bg_ref[...]
        for d, (a_ref, u_ref) in enumerate(((af_ref, uf_ref), (ab_ref, ub_ref))):
            r = _sigmoid(g[:, (2 * d) * LANES:(2 * d + 1) * LANES])
            gi = _sigmoid(g[:, (2 * d + 1) * LANES:(2 * d + 2) * LANES])
            log_a = r * decay[d:d + 1, :]
            a_ref[pl.ds(r0, chunk), :] = jnp.exp(log_a)
            th = jnp.tanh(log_a)
            w = -2.0 * th
            mult = jnp.where(w > 0.0, w * lax.rsqrt(w * (1.0 - th)), 0.0)
            u_ref[pl.ds(r0, chunk), :] = mult * (gi * xc)
        return c

    lax.fori_loop(0, n_chunks, gates, 0)

    def step(t, carry):
        hf, hb = carry
        rf = pl.multiple_of(t * BG, BG)
        rb = pl.multiple_of((seq - 1 - t) * BG, BG)
        hf = af_ref[pl.ds(rf, BG), :] * hf + uf_ref[pl.ds(rf, BG), :]
        y_ref[pl.ds(rf, BG), :] = hf
        hb = ab_ref[pl.ds(rb, BG), :] * hb + ub_ref[pl.ds(rb, BG), :]
        xpad_ref[pl.ds(rb, BG), :] = hb
        return hf, hb

    hf, hb = lax.fori_loop(0, seq, step, (h0_ref[0], h0_ref[1]), unroll=8)
    fin_ref[0] = hf
    fin_ref[1] = hb

    def emit(i, c):
        r0 = pl.multiple_of(i * chunk, chunk)
        gate = jax.nn.gelu(rg_ref[pl.ds(r0, chunk), :])
        y_ref[pl.ds(r0, chunk), :] = gate * (y_ref[pl.ds(r0, chunk), :] + xpad_ref[pl.ds(r0, chunk), :])
        return c

    lax.fori_loop(0, n_chunks, emit, 0)


def _rglru(rx, rg, pp, l, h0, chunk=512):
    nbg, _, n, _ = rx.shape
    seq = n // BG
    slab = pl.BlockSpec((None, None, n, LANES), lambda g, c: (g, c, 0, 0))
    col = lambda rows: pl.BlockSpec((None, rows, LANES), lambda g, c: (l, 0, c))
    state = pl.BlockSpec((N_DIR, BG, LANES), lambda g, c: (0, g, c))
    return pl.pallas_call(
        functools.partial(_rglru_kernel, seq=seq, chunk=chunk),
        grid=(nbg, N_SLAB),
        in_specs=[slab, slab, col(CONV_W), col(1),
                  pl.BlockSpec((None, None, LANES, 4 * LANES), lambda g, c: (l, c, 0, 0)),
                  pl.BlockSpec((None, None, 1, 4 * LANES), lambda g, c: (l, c, 0, 0)),
                  col(N_DIR), state],
        out_specs=[slab, state],
        out_shape=[jax.ShapeDtypeStruct(rx.shape, F32), jax.ShapeDtypeStruct((N_DIR, nbg * BG, D_RNN), F32)],
        scratch_shapes=[pltpu.VMEM((n + 2 * CONV_LEFT * BG, LANES), F32)] + [pltpu.VMEM((n, LANES), F32)] * 4,
        compiler_params=_cparams(("arbitrary", "arbitrary")),
        name="rglru_scan",
    )(rx, rg, pp['conv_w'], pp['conv_b'], pp['wg'], pp['bg'], pp['lam'], h0)


def _s5_param_kernel(lr_ref, li_ref, ls_ref, br_ref, bi_ref, are_ref, aim_ref, bbr_ref, bbi_ref):
    lr = lr_ref[...]
    li = li_ref[...]
    step = jnp.exp(ls_ref[...])
    mag = jnp.exp(lr * step)
    ab_re = mag * jnp.cos(li * step)
    ab_im = mag * jnp.sin(li * step)
    den = lr * lr + li * li
    nr = ab_re - 1.0
    ni = ab_im
    f_re = (nr * lr + ni * li) / den
    f_im = (ni * lr - nr * li) / den
    are_ref[...] = ab_re
    aim_ref[...] = ab_im
    br = br_ref[...]
    bi = bi_ref[...]
    bbr_ref[...] = f_re[:, None, :] * br - f_im[:, None, :] * bi
    bbi_ref[...] = f_re[:, None, :] * bi + f_im[:, None, :] * br


def _s5_params(lam_re, lam_im, log_step, b_re, b_im):
    rows = DEPTH * N_DIR * SSM_GROUPS
    flat = lambda a: a.reshape(rows, SSM_STATE)
    bt = lambda a: a.reshape(rows, SSM_STATE, SSM_GROUP_W).transpose(0, 2, 1)
    vec = jax.ShapeDtypeStruct((rows, SSM_STATE), F32)
    mat = jax.ShapeDtypeStruct((rows, SSM_GROUP_W, SSM_STATE), F32)
    return pl.pallas_call(
        _s5_param_kernel,
        out_shape=[vec, vec, mat, mat],
        name="s5_discretise",
    )(flat(lam_re), flat(lam_im), log_step.reshape(rows, 1), bt(b_re), bt(b_im))


def _s5_kernel(uf_ref, ub_ref, bm_ref, cm_ref, a_ref, s0_ref, yf_ref, yb_ref, fin_ref, bu_ref, xs_ref, st_ref,
               *, pairs_per_pass):
    rows = uf_ref.shape[1]
    steps = rows // BG
    i = pl.program_id(1)

    @pl.when(i == 0)
    def _():
        st_ref[...] = s0_ref[...]

    for d, u_ref in enumerate((uf_ref, ub_ref)):
        for s in range(SSM_SG):
            bu_ref[d, :, s * 2 * SG_STATE:(s + 1) * 2 * SG_STATE] = _dot(u_ref[s].astype(BF16), bm_ref[d, s])

    pair_cols = [(s * 2 * SG_STATE + j * LANES, s * 2 * SG_STATE + SG_STATE + j * LANES)
                 for s in range(SSM_SG) for j in range(SG_STATE // LANES)]
    for p0 in range(0, len(pair_cols), pairs_per_pass):
        cols = pair_cols[p0:p0 + pairs_per_pass]

        def load_state(d):
            return tuple((st_ref[d, :, cr:cr + LANES], st_ref[d, :, ci:ci + LANES]) for cr, ci in cols)

        def step(t, carry):
            new = []
            for d in range(N_DIR):
                tt = t if d == 0 else steps - 1 - t
                r0 = pl.multiple_of(tt * BG, BG)
                new_d = []
                for (cr, ci), (xr, xi) in zip(cols, carry[d]):
                    ar = a_ref[d, :, cr:cr + LANES]
                    ai = a_ref[d, :, ci:ci + LANES]
                    nr = ar * xr - ai * xi + bu_ref[d, pl.ds(r0, BG), cr:cr + LANES]
                    ni = ar * xi + ai * xr + bu_ref[d, pl.ds(r0, BG), ci:ci + LANES]
                    xs_ref[d, pl.ds(r0, BG), cr:cr + LANES] = nr
                    xs_ref[d, pl.ds(r0, BG), ci:ci + LANES] = ni
                    new_d.append((nr, ni))
                new.append(tuple(new_d))
            return tuple(new)

        final = lax.fori_loop(0, steps, step, (load_state(0), load_state(1)), unroll=2)
        for d in range(N_DIR):
            for (cr, ci), (xr, xi) in zip(cols, final[d]):
                st_ref[d, :, cr:cr + LANES] = xr
                st_ref[d, :, ci:ci + LANES] = xi

    for d, y_ref in enumerate((yf_ref, yb_ref)):
        for s in range(SSM_SG):
            xs = xs_ref[d, :, s * 2 * SG_STATE:(s + 1) * 2 * SG_STATE].astype(BF16)
            y_ref[s] = _dot(xs, cm_ref[d, s])

    @pl.when(i == pl.num_programs(1) - 1)
    def _():
        fin_ref[...] = st_ref[...]


def _s5(su, pp, l, s0, rows=512):
    nbg, _, n, _ = su.shape
    nt = n // rows
    fwd = pl.BlockSpec((None, N_SLAB, rows, LANES), lambda g, i: (g, 0, i, 0))
    bwd = pl.BlockSpec((None, N_SLAB, rows, LANES), lambda g, i: (g, 0, nt - 1 - i, 0))
    st = pl.BlockSpec((N_DIR, BG, SSM_FLAT), lambda g, i: (0, g, 0))
    return pl.pallas_call(
        functools.partial(_s5_kernel, pairs_per_pass=4),
        grid=(nbg, nt),
        in_specs=[fwd, bwd, _const((N_DIR, SSM_SG, LANES, 2 * SG_STATE), (l,), single=True),
                  _const((N_DIR, SSM_SG, 2 * SG_STATE, LANES), (l,), single=True),
                  _const((N_DIR, BG, SSM_FLAT), (l,)), st],
        out_specs=[fwd, bwd, st],
        out_shape=[jax.ShapeDtypeStruct(su.shape, F32)] * 2 + [jax.ShapeDtypeStruct((N_DIR, nbg * BG, SSM_FLAT), F32)],
        scratch_shapes=[pltpu.VMEM((N_DIR, rows, SSM_FLAT), F32), pltpu.VMEM((N_DIR, rows, SSM_FLAT), F32),
                        pltpu.VMEM((N_DIR, BG, SSM_FLAT), F32)],
        compiler_params=_cparams(("arbitrary", "arbitrary")),
        name="s5_scan",
    )(su, su, pp['bm'], pp['cm'], pp['a_flat'], s0)


def _route(logits):
    rows = logits.shape[0]
    lane = lax.broadcasted_iota(jnp.int32, (rows, ROUTE_W), 1)
    neg_inf = -jnp.inf
    big = ROUTE_W

    def row_max(v):
        return jnp.max(v, axis=-1, keepdims=True)

    def first_lane(mask):
        return jnp.min(jnp.where(mask, lane, big), axis=-1, keepdims=True)

    in_g = lane < N_EXPERT_GROUPS
    gl = jnp.where(in_g, logits, neg_inf)
    g_max = row_max(gl)
    g_exp = jnp.where(in_g, jnp.exp(gl - g_max), 0.0)
    g_w = 1.0 / jnp.sum(g_exp, axis=-1, keepdims=True)
    g_sel = first_lane(gl == g_max)
    e_lo = N_EXPERT_GROUPS + g_sel * EXPERTS_PER_GROUP
    in_e = (lane >= e_lo) & (lane < e_lo + EXPERTS_PER_GROUP)
    el = jnp.where(in_e, logits, neg_inf)
    v1 = row_max(el)
    i1 = first_lane(el == v1)
    el2 = jnp.where(lane == i1, neg_inf, el)
    v2 = row_max(el2)
    i2 = first_lane(el2 == v2)
    e2 = jnp.exp(v2 - v1)
    w1 = g_w / (1.0 + e2)
    w2 = g_w * e2 / (1.0 + e2)
    return jnp.where(lane == i1, w1, jnp.where(lane == i2, w2, 0.0))


def _merge_kernel(x_ref, o_ref, yr_ref, su_ref, yf_ref, yb_ref, mod_ref, g1_ref, g2_ref, wgl_ref, wa_ref, wr_ref,
                  dsk_ref, wglu_ref, wout_ref, wrt_ref, brt_ref, x1_ref, h2_ref, comb_ref,
                  h_scr, yr_scr, ys_scr, h2_scr):
    tt = x_ref.shape[1]
    rows = BG * tt
    g1 = g1_ref[...]
    g2 = g2_ref[...]
    dsk = dsk_ref[...]
    for b in range(BG):
        rb = slice(b * tt, (b + 1) * tt)
        h_scr[rb, :] = _modulate(x_ref[b], g1, mod_ref, b, 0, 1).astype(BF16)
        tm_rows = _strided_rows(b, tt)
        for c in range(N_SLAB):
            cs = slice(c * LANES, (c + 1) * LANES)
            yr_scr[rb, cs] = yr_ref.at[c][tm_rows, :].astype(BF16)
            ys = dsk[:, cs] * su_ref.at[c][tm_rows, :] + yf_ref.at[c][tm_rows, :] + yb_ref.at[c][tm_rows, :]
            ys_scr[rb, cs] = jax.nn.gelu(ys).astype(BF16)
    h = h_scr[...]
    y_attn = _dot(o_ref[...].reshape(rows, ATTN_WIDTH).astype(BF16), wa_ref[...])
    mix = _sigmoid(_dot(h, wgl_ref[:, 0:D_MODEL])) * y_attn
    y_rnn = _dot(yr_scr[...], wr_ref[...])
    mix = mix + _sigmoid(_dot(h, wgl_ref[:, D_MODEL:2 * D_MODEL])) * y_rnn
    glu = _dot(ys_scr[...], wglu_ref[...])
    y_ssm = glu[:, :D_MODEL] * _sigmoid(glu[:, D_MODEL:])
    mix = mix + _sigmoid(_dot(h, wgl_ref[:, 2 * D_MODEL:3 * D_MODEL])) * y_ssm
    mo = _dot(mix.astype(BF16), wout_ref[...])
    for b in range(BG):
        rb = slice(b * tt, (b + 1) * tt)
        x1 = x_ref[b] + mo[rb, :] * mod_ref[b, 2:3, :]
        x1_ref[b] = x1
        h2 = _modulate(x1, g2, mod_ref, b, 3, 4)
        h2_ref[b] = h2.astype(BF16)
        h2_scr[rb, :] = h2
    comb = _route(_dot3(h2_scr[...], wrt_ref[...]) + brt_ref[...])
    comb_ref[...] = comb.reshape(BG, tt, ROUTE_W)


def _merge(x, o, yr, su, yf, yb, mod, pp, l, tt=64):
    batch, seq, _ = x.shape
    nbg = batch // BG
    rows = BG * tt
    tok = lambda w: pl.BlockSpec((BG, tt, w), lambda g, i: (g, i, 0))
    slab = pl.BlockSpec((None, N_SLAB, rows, LANES), lambda g, i: (g, 0, i, 0))
    wspec = lambda *shape: _const(shape, (l,), single=True)
    return pl.pallas_call(
        _merge_kernel,
        grid=(nbg, seq // tt),
        in_specs=[tok(D_MODEL), tok(ATTN_WIDTH), slab, slab, slab, slab,
                  _const((BG, N_MOD, D_MODEL)), _const((1, D_MODEL), (l,)), _const((1, D_MODEL), (l,)),
                  wspec(D_MODEL, 3 * D_MODEL), wspec(ATTN_WIDTH, D_MODEL), wspec(D_RNN, D_MODEL),
                  _const((1, D_SSM), (l,)), wspec(D_SSM, 2 * D_MODEL), wspec(D_MODEL, D_MODEL),
                  wspec(D_MODEL, ROUTE_W), _const((1, ROUTE_W), (l,))],
        out_specs=[tok(D_MODEL), tok(D_MODEL), tok(ROUTE_W)],
        out_shape=[jax.ShapeDtypeStruct((batch, seq, D_MODEL), F32), jax.ShapeDtypeStruct((batch, seq, D_MODEL), BF16),
                   jax.ShapeDtypeStruct((batch, seq, ROUTE_W), F32)],
        scratch_shapes=[pltpu.VMEM((rows, D_MODEL), BF16), pltpu.VMEM((rows, D_RNN), BF16),
                        pltpu.VMEM((rows, D_SSM), BF16), pltpu.VMEM((rows, D_MODEL), F32)],
        compiler_params=_cparams(("arbitrary", "arbitrary")),
        name="mixer_merge",
    )(x, o, yr, su, yf, yb, mod, pp['norm1'], pp['norm2'], pp['w_gl'], pp['w_attn_o'], pp['w_rnn_o'],
      pp['d_skip'], pp['w_glu'], pp['w_out'], pp['w_rt'], pp['b_rt'])


def _moe_kernel(x1_ref, h2_ref, comb_ref, mod_ref, wg_ref, wu_ref, wd_ref, *rest, final):
    if final:
        gf_ref, x2_ref, y_ref, acc_ref = rest
    else:
        x2_ref, acc_ref = rest
    e = pl.program_id(1)

    @pl.when(e == 0)
    def _():
        acc_ref[...] = jnp.zeros_like(acc_ref)

    comb = comb_ref[...]
    lane = lax.broadcasted_iota(jnp.int32, comb.shape, 1)
    w_e = jnp.sum(jnp.where(lane == e + N_EXPERT_GROUPS, comb, 0.0), axis=-1, keepdims=True)
    h2 = h2_ref[...]
    gate = _dot(h2, wg_ref[...])
    act = gate * _sigmoid(gate) * _dot(h2, wu_ref[...]) * w_e
    acc_ref[...] += _dot(act.astype(BF16), wd_ref[...])

    @pl.when(e == N_EXPERTS - 1)
    def _():
        x2 = x1_ref[...] + acc_ref[...] * mod_ref[5:6, :]
        x2_ref[...] = x2
        if final:
            y_ref[...] = _rms(x2, gf_ref[...])


def _moe(x1, h2, comb, mod, pp, l, seq, g_final, tm=1024):
    n = x1.shape[0]
    final = g_final is not None
    per_batch = mod.shape[0] > 1
    assert not per_batch or seq % tm == 0
    mod_idx = (lambda i, e: (i * tm // seq, 0, 0)) if per_batch else (lambda i, e: (0, 0, 0))
    rows = lambda w: pl.BlockSpec((tm, w), lambda i, e: (i, 0))
    in_specs = [rows(D_MODEL), rows(D_MODEL), rows(ROUTE_W),
                pl.BlockSpec((None, N_MOD, D_MODEL), mod_idx),
                pl.BlockSpec((None, None, D_MODEL, D_EXPERT), lambda i, e: (l, e, 0, 0)),
                pl.BlockSpec((None, None, D_MODEL, D_EXPERT), lambda i, e: (l, e, 0, 0)),
                pl.BlockSpec((None, None, D_EXPERT, D_MODEL), lambda i, e: (l, e, 0, 0))]
    args = [x1, h2, comb, mod, pp['w_gate'], pp['w_up'], pp['w_down']]
    out_specs = [rows(D_MODEL)]
    out_shape = [jax.ShapeDtypeStruct((n, D_MODEL), F32)]
    if final:
        in_specs.append(pl.BlockSpec((1, D_MODEL), lambda i, e: (0, 0)))
        args.append(g_final)
        out_specs.append(rows(D_MODEL))
        out_shape.append(jax.ShapeDtypeStruct((n, D_MODEL), F32))
    return pl.pallas_call(
        functools.partial(_moe_kernel, final=final),
        grid=(n // tm, N_EXPERTS),
        in_specs=in_specs,
        out_specs=out_specs,
        out_shape=out_shape,
        scratch_shapes=[pltpu.VMEM((tm, D_MODEL), F32)],
        compiler_params=_cparams(("arbitrary", "arbitrary")),
        name="hier_moe",
    )(*args)


def _head_mean_matrix(width):
    idx = np.arange(width) // HEAD_DIM
    return jnp.asarray((idx[:, None] == idx[None, :]).astype(np.float32) / HEAD_DIM, dtype=BF16)


def _rope_tables(seq):
    t = np.arange(seq)
    row = (t // GRID_W).astype(np.float64)
    col = (t % GRID_W).astype(np.float64)
    inv = ROPE_BASE ** (-np.arange(ROPE_PAIRS_PER_AXIS, dtype=np.float64) / ROPE_PAIRS_PER_AXIS)
    inv = inv.astype(np.float32).astype(np.float64)
    ang = np.concatenate([row[:, None] * inv, col[:, None] * inv], axis=-1)
    ang = ang.astype(np.float32).astype(np.float64)
    cos = np.repeat(np.cos(ang), 2, axis=-1)
    sin = np.repeat(np.sin(ang), 2, axis=-1) * np.tile(np.array([-1.0, 1.0]), HEAD_DIM // 2)
    expand = lambda a: jnp.asarray(np.tile(a, (1, LANES // HEAD_DIM)), dtype=F32)
    return expand(cos), expand(sin)


def _prep_params(p, s5p):
    bf = lambda a: a.astype(BF16)
    pp = {}
    pp['w_proj'] = bf(p['w_in'][:, :, :PROJ_W])
    pp['w_gl'] = bf(p['w_in'][:, :, PROJ_W:])
    pp['norm1'] = p['norm1'][:, None, :]
    pp['norm2'] = p['norm2'][:, None, :]
    pp['qg'] = jnp.tile(p['q_norm'], (1, N_HEADS))[:, None, :]
    pp['kg'] = jnp.tile(p['k_norm'], (1, N_KV_HEADS))[:, None, :]
    pp['s_q'] = _head_mean_matrix(ATTN_WIDTH)
    pp['s_k'] = _head_mean_matrix(KV_WIDTH)
    pp['w_attn_o'] = bf(p['w_attn_o'])
    pp['w_rnn_o'] = bf(p['w_rnn_o'])
    pp['w_glu'] = bf(p['w_glu'])
    pp['w_out'] = bf(p['w_out'])
    pp['d_skip'] = p['ssm_d'][:, None, :]
    pp['conv_w'] = p['conv_w']
    pp['conv_b'] = p['conv_b'][:, None, :]
    pp['lam'] = p['lru_lambda']
    per = LANES // RNN_BLOCK_W
    wa, wi = p['lru_w_a'], p['lru_w_i']
    w = jnp.stack([wa[:, 0], wi[:, 0], wa[:, 1], wi[:, 1]], axis=1)
    w = w.reshape(DEPTH, 4, N_SLAB, per, RNN_BLOCK_W, RNN_BLOCK_W)
    eye = jnp.eye(per, dtype=F32)
    wg = w.transpose(0, 2, 3, 4, 1, 5)[:, :, :, :, :, None, :] * eye[None, None, :, None, None, :, None]
    pp['wg'] = bf(wg.reshape(DEPTH, N_SLAB, LANES, 4 * LANES))
    ba, bi = p['lru_b_a'], p['lru_b_i']
    b = jnp.stack([ba[:, 0], bi[:, 0], ba[:, 1], bi[:, 1]], axis=1).reshape(DEPTH, 4, N_SLAB, LANES)
    pp['bg'] = b.transpose(0, 2, 1, 3).reshape(DEPTH, N_SLAB, 1, 4 * LANES)
    ab_re, ab_im, bb_re, bb_im = s5p
    eye_g = jnp.eye(SG_GROUPS, dtype=F32)
    lead = (DEPTH, N_DIR, SSM_SG, SG_GROUPS)
    bb = jnp.stack([bb_re, bb_im], axis=2).reshape(lead + (SSM_GROUP_W, 2, SSM_STATE))
    bm = bb[:, :, :, :, :, :, None, :] * eye_g[None, None, None, :, None, None, :, None]
    pp['bm'] = bf(bm.reshape(DEPTH, N_DIR, SSM_SG, LANES, 2 * SG_STATE))
    cc = jnp.stack([p['ssm_c_re'], -p['ssm_c_im']], axis=2)
    cc = cc.reshape(DEPTH, N_DIR, 2, SSM_SG, SG_GROUPS, SSM_GROUP_W, SSM_STATE).transpose(0, 1, 3, 2, 4, 6, 5)
    cm = cc[:, :, :, :, :, :, None, :] * eye_g[None, None, None, None, :, None, :, None]
    pp['cm'] = bf(cm.reshape(DEPTH, N_DIR, SSM_SG, 2 * SG_STATE, LANES))
    are = ab_re.reshape(DEPTH, N_DIR, SSM_SG, SG_STATE)
    aim = ab_im.reshape(DEPTH, N_DIR, SSM_SG, SG_STATE)
    a_flat = jnp.concatenate([are, aim], axis=-1).reshape(DEPTH, N_DIR, 1, SSM_FLAT)
    pp['a_flat'] = jnp.broadcast_to(a_flat, (DEPTH, N_DIR, BG, SSM_FLAT))
    fill = ROUTE_W - N_EXPERT_GROUPS - N_EXPERTS
    pp['w_rt'] = jnp.concatenate([p['router_g_w'], p['router_e_w'], jnp.zeros((DEPTH, D_MODEL, fill), F32)], axis=-1)
    pp['b_rt'] = jnp.concatenate([p['router_g_b'], p['router_e_b'], jnp.zeros((DEPTH, fill), F32)], axis=-1)[:, None, :]
    pp['w_gate'] = bf(p['w_e_gate'])
    pp['w_up'] = bf(p['w_e_up'])
    pp['w_down'] = bf(p['w_e_down'])
    return pp


def _flat_state(re, im):
    b = re.shape[0]
    r = re.transpose(1, 2, 0, 3, 4).reshape(DEPTH, N_DIR, b, SSM_SG, SG_STATE)
    i = im.transpose(1, 2, 0, 3, 4).reshape(DEPTH, N_DIR, b, SSM_SG, SG_STATE)
    return jnp.concatenate([r, i], axis=-1).reshape(DEPTH, N_DIR, b, SSM_FLAT)


def _unflat_state(flat):
    b = flat.shape[2]
    f = flat.reshape(DEPTH, N_DIR, b, SSM_SG, 2, SG_GROUPS, SSM_STATE).transpose(4, 2, 0, 1, 3, 5, 6)
    f = f.reshape(2, b, DEPTH, N_DIR, SSM_GROUPS, SSM_STATE)
    return f[0], f[1]


def _trunk_layer(x, mod, mod_moe, pp, l, ctx, rope_tabs, g_final, attn_tq):
    batch, seq, _ = x.shape
    q, k, v, rx, rg, su = _in_proj(x, mod, pp, l, rope_tabs)
    if ctx is None:
        attn_ctx = None
        h0 = jnp.zeros((N_DIR, batch, D_RNN), F32)
        s0 = jnp.zeros((N_DIR, batch, SSM_FLAT), F32)
    else:
        cache_k, cache_v, h0, s0 = ctx
        attn_ctx = (cache_k, cache_v, l)
    o = _attention(q, k, v, attn_ctx, attn_tq)
    yr, rnn_fin = _rglru(rx, rg, pp, l, h0)
    yf, yb, ssm_fin = _s5(su, pp, l, s0)
    x1, h2, comb = _merge(x, o, yr, su, yf, yb, mod, pp, l)
    flat = lambda a: a.reshape(batch * seq, a.shape[-1])
    outs = _moe(flat(x1), flat(h2), flat(comb), mod_moe, pp, l, seq, g_final)
    return [a.reshape(batch, seq, D_MODEL) for a in outs], (k, v, rnn_fin, ssm_fin)


def kernel(x_prompt, x_sample, cache_k, cache_v, state_rglru, state_ssm_re, state_ssm_im, c, c_ctx, w_mod, b_mod, norm1, norm2, w_in, q_norm, k_norm, w_attn_o, conv_w, conv_b, lru_w_a, lru_b_a, lru_w_i, lru_b_i, lru_lambda, w_rnn_o, ssm_lam_re, ssm_lam_im, ssm_log_step, ssm_b_re, ssm_b_im, ssm_c_re, ssm_c_im, ssm_d, w_glu, w_out, router_g_w, router_g_b, router_e_w, router_e_b, w_e_gate, w_e_up, w_e_down, final_norm):
    p = dict(norm1=norm1, norm2=norm2, w_in=w_in, q_norm=q_norm, k_norm=k_norm, w_attn_o=w_attn_o, conv_w=conv_w,
             conv_b=conv_b, lru_w_a=lru_w_a, lru_b_a=lru_b_a, lru_w_i=lru_w_i, lru_b_i=lru_b_i,
             lru_lambda=lru_lambda, w_rnn_o=w_rnn_o, ssm_c_re=ssm_c_re, ssm_c_im=ssm_c_im, ssm_d=ssm_d, w_glu=w_glu,
             w_out=w_out, router_g_w=router_g_w, router_g_b=router_g_b, router_e_w=router_e_w,
             router_e_b=router_e_b, w_e_gate=w_e_gate, w_e_up=w_e_up, w_e_down=w_e_down)
    bc, lc, _ = x_prompt.shape
    bd, ld, _ = x_sample.shape
    past = cache_k.shape[2]
    assert bd == BG and bc % BG == 0

    n_cond = 1 + bd
    cond_rows = -(-n_cond // SUBLANES) * SUBLANES
    cond = jnp.concatenate([c_ctx[None], c, jnp.zeros((cond_rows - n_cond, D_MODEL), F32)], axis=0)
    mods = _modulation(cond, w_mod, b_mod).reshape(DEPTH, cond_rows, N_MOD, D_MODEL)
    s5p = _s5_params(ssm_lam_re, ssm_lam_im, ssm_log_step, ssm_b_re, ssm_b_im)
    pp = _prep_params(p, s5p)
    rope_tabs = _rope_tables(ld)
    final_g = final_norm[None]

    ck = cache_k.reshape(bd, DEPTH, past, KV_WIDTH)
    cv = cache_v.reshape(bd, DEPTH, past, KV_WIDTH)
    h0_lat = state_rglru.transpose(1, 2, 0, 3)
    s0_lat = _flat_state(state_ssm_re, state_ssm_im)
    x_ctx, x_lat = x_prompt, x_sample
    ks, vs, rs, ss = [], [], [], []
    y_ctx = y_lat = None
    for l in range(DEPTH):
        g_final = final_g if l == DEPTH - 1 else None
        mod_ctx1 = mods[l, 0:1]
        mod_ctx = jnp.broadcast_to(mod_ctx1, (BG, N_MOD, D_MODEL))
        mod_lat = mods[l, 1:n_cond]
        outs, (k_l, v_l, r_l, s_l) = _trunk_layer(x_ctx, mod_ctx, mod_ctx1, pp, l, None, None, g_final, lc)
        x_ctx = outs[0]
        if g_final is not None:
            y_ctx = outs[1]
        ks.append(k_l)
        vs.append(v_l)
        rs.append(r_l)
        ss.append(s_l)
        ctx = (ck, cv, h0_lat[l], s0_lat[l])
        outs, _ = _trunk_layer(x_lat, mod_lat, mod_lat, pp, l, ctx, rope_tabs, g_final, 512)
        x_lat = outs[0]
        if g_final is not None:
            y_lat = outs[1]
    new_k = jnp.stack(ks, axis=1).reshape(bc, DEPTH, lc, N_KV_HEADS, HEAD_DIM)
    new_v = jnp.stack(vs, axis=1).reshape(bc, DEPTH, lc, N_KV_HEADS, HEAD_DIM)
    new_r = jnp.stack(rs, axis=0).transpose(2, 0, 1, 3)
    new_re, new_im = _unflat_state(jnp.stack(ss, axis=0))
    return (y_ctx, y_lat, new_k, new_v, new_r, new_re, new_im)
```

```python
import functools

import numpy as np
import jax
import jax.numpy as jnp
from jax import lax
from jax.experimental import pallas as pl
from jax.experimental.pallas import tpu as pltpu

F32 = jnp.float32
BF16 = jnp.bfloat16

D_MODEL = 1024
DEPTH = 2
GRID_W = 64
N_DIR = 2
NORM_EPS = 1e-6
N_MOD = 6

N_HEADS = 8
N_KV_HEADS = 4
HEAD_DIM = 64
ATTN_WIDTH = N_HEADS * HEAD_DIM
KV_WIDTH = N_KV_HEADS * HEAD_DIM
ROPE_BASE = 10000.0
ROPE_PAIRS_PER_AXIS = HEAD_DIM // 4

D_RNN = D_MODEL // 2
RNN_BLOCKS = 8
RNN_BLOCK_W = D_RNN // RNN_BLOCKS
CONV_W = 4
CONV_LEFT = 2
LRU_C = 8.0

D_SSM = D_MODEL // 2
SSM_GROUP_W = 16
SSM_GROUPS = D_SSM // SSM_GROUP_W
SSM_STATE = 64

N_EXPERT_GROUPS = 4
EXPERTS_PER_GROUP = 4
N_EXPERTS = N_EXPERT_GROUPS * EXPERTS_PER_GROUP
D_EXPERT = D_MODEL // 4

LANES = 128
SUBLANES = 8
VMEM_LIMIT = 56 * 1024 * 1024

BG = SUBLANES
PROJ_W = ATTN_WIDTH + 2 * KV_WIDTH + 2 * D_RNN + D_SSM
N_SLAB = D_RNN // LANES
GATE_BLK = 512
N_GATE_BLK = 3 * D_MODEL // GATE_BLK
assert PROJ_W % GATE_BLK == 0

SG_GROUPS = LANES // SSM_GROUP_W
SSM_SG = SSM_GROUPS // SG_GROUPS
SG_STATE = SG_GROUPS * SSM_STATE
SSM_FLAT = SSM_SG * 2 * SG_STATE

ROUTE_W = LANES
EXPERTS_PER_STEP = 4


def _cparams(sem, vmem=VMEM_LIMIT):
    return pltpu.CompilerParams(dimension_semantics=sem, vmem_limit_bytes=vmem)


def _const(shape, lead=(), single=False):
    idx = tuple(lead) + (0,) * len(shape)
    block = (None,) * len(lead) + tuple(shape)
    if single:
        return pl.BlockSpec(block, lambda *_: idx, pipeline_mode=pl.Buffered(1))
    return pl.BlockSpec(block, lambda *_: idx)


def _dot(a, b):
    return jnp.dot(a, b, preferred_element_type=F32)


def _dot_nt(a, b):
    return lax.dot_general(a, b, (((1,), (1,)), ((), ())), preferred_element_type=F32)


def _split_bf16(a):
    hi = a.astype(BF16)
    lo = (a - hi.astype(F32)).astype(BF16)
    return hi, lo


def _dot3(a, b):
    a_hi, a_lo = _split_bf16(a)
    b_hi, b_lo = _split_bf16(b)
    return _dot(a_hi, b_hi) + (_dot(a_lo, b_hi) + _dot(a_hi, b_lo))


def _sigmoid(x):
    return 0.5 * jnp.tanh(0.5 * x) + 0.5


def _rms(x, g):
    ms = jnp.mean(x * x, axis=-1, keepdims=True)
    return x * lax.rsqrt(ms + NORM_EPS) * g


def _modulate(x, g, mod_ref, b, shift_i, scale_i):
    return _rms(x, g) * (1.0 + mod_ref[b, scale_i:scale_i + 1, :]) + mod_ref[b, shift_i:shift_i + 1, :]


def _strided_rows(b, n):
    return pl.ds(b, n, stride=BG)


def _mod_kernel(c_ref, w_ref, b_ref, o_ref):
    c = c_ref[...]
    a = c * _sigmoid(c)
    o_ref[...] = _dot3(a, w_ref[...]) + b_ref[...]


def _modulation(cond, w_mod, b_mod):
    rows = cond.shape[0]
    tn = 1536
    width = N_MOD * D_MODEL
    return pl.pallas_call(
        _mod_kernel,
        grid=(DEPTH, width // tn),
        in_specs=[
            pl.BlockSpec((rows, D_MODEL), lambda l, j: (0, 0)),
            pl.BlockSpec((None, D_MODEL, tn), lambda l, j: (l, 0, j)),
            pl.BlockSpec((None, 1, tn), lambda l, j: (l, 0, j)),
        ],
        out_specs=pl.BlockSpec((None, rows, tn), lambda l, j: (l, 0, j)),
        out_shape=jax.ShapeDtypeStruct((DEPTH, rows, width), F32),
        compiler_params=_cparams(("arbitrary", "arbitrary")),
        name="adaln_modulation",
    )(cond, w_mod, b_mod.reshape(DEPTH, 1, width))


def _head_norm(z, s_ref, gain):
    ms = _dot((z * z).astype(BF16), s_ref[...])
    return z * lax.rsqrt(ms + NORM_EPS) * gain


def _rope(z, cos, sin):
    rows = z.shape[0]
    tt = cos.shape[0]
    lane = lax.broadcasted_iota(jnp.int32, (rows, LANES), 1)
    partner = jnp.where((lane & 1) == 0, pltpu.roll(z, LANES - 1, 1), pltpu.roll(z, 1, 1))
    z3 = z.reshape(rows // tt, tt, LANES)
    p3 = partner.reshape(rows // tt, tt, LANES)
    return z3 * cos[None] + p3 * sin[None]


def _in_proj_kernel(x_ref, mod_ref, g_ref, w_ref, qg_ref, kg_ref, sq_ref, sk_ref, *rest, rope):
    if rope:
        cos_ref, sin_ref, q_ref, k_ref, v_ref, rx_ref, rg_ref, su_ref, h_scr = rest
    else:
        q_ref, k_ref, v_ref, rx_ref, rg_ref, su_ref, h_scr = rest
    tt = x_ref.shape[1]
    g = g_ref[...]
    qscale = HEAD_DIM ** -0.5
    for b in range(BG):
        h_scr[b * tt:(b + 1) * tt, :] = _modulate(x_ref[b], g, mod_ref, b, 0, 1).astype(BF16)
    h = h_scr[...]

    def proj(lo, width):
        return _dot(h, w_ref[:, lo:lo + width])

    q = _head_norm(proj(0, ATTN_WIDTH), sq_ref, qg_ref[...]) * qscale
    k = _head_norm(proj(ATTN_WIDTH, KV_WIDTH), sk_ref, kg_ref[...])
    if rope:
        cos = cos_ref[...]
        sin = sin_ref[...]
        for dst, z in ((q_ref, q), (k_ref, k)):
            for c in range(z.shape[1] // LANES):
                cs = slice(c * LANES, (c + 1) * LANES)
                dst[:, :, cs] = _rope(z[:, cs], cos, sin)
    else:
        q_ref[...] = q.reshape(BG, tt, ATTN_WIDTH)
        k_ref[...] = k.reshape(BG, tt, KV_WIDTH)
    off = ATTN_WIDTH + KV_WIDTH
    v_ref[...] = proj(off, KV_WIDTH).reshape(BG, tt, KV_WIDTH)
    off += KV_WIDTH
    for dst in (rx_ref, rg_ref, su_ref):
        z = proj(off, D_RNN)
        off += D_RNN
        for b in range(BG):
            for c in range(N_SLAB):
                dst.at[c][_strided_rows(b, tt), :] = z[b * tt:(b + 1) * tt, c * LANES:(c + 1) * LANES]


def _in_proj(x, mod, pp, l, rope_tabs, tt=64):
    batch, seq, _ = x.shape
    nbg = batch // BG
    rope = rope_tabs is not None
    tok = lambda w: pl.BlockSpec((BG, tt, w), lambda g, i: (g, i, 0))
    slab = pl.BlockSpec((None, N_SLAB, tt * BG, LANES), lambda g, i: (g, 0, i, 0))
    in_specs = [tok(D_MODEL), _const((BG, N_MOD, D_MODEL)), _const((1, D_MODEL), (l,)),
                pl.BlockSpec((None, D_MODEL, PROJ_W), lambda *_: (l, 0, 0), pipeline_mode=pl.Buffered(1)),
                _const((1, ATTN_WIDTH), (l,)),
                _const((1, KV_WIDTH), (l,)), _const((ATTN_WIDTH, ATTN_WIDTH)), _const((KV_WIDTH, KV_WIDTH))]
    args = [x, mod, pp['norm1'], pp['w_in'], pp['qg'], pp['kg'], pp['s_q'], pp['s_k']]
    if rope:
        tab = pl.BlockSpec((tt, LANES), lambda g, i: (i, 0))
        in_specs += [tab, tab]
        args += list(rope_tabs)
    tok_shape = lambda w: jax.ShapeDtypeStruct((batch, seq, w), F32)
    slab_shape = jax.ShapeDtypeStruct((nbg, N_SLAB, seq * BG, LANES), F32)
    return pl.pallas_call(
        functools.partial(_in_proj_kernel, rope=rope),
        grid=(nbg, seq // tt),
        in_specs=in_specs,
        out_specs=[tok(ATTN_WIDTH), tok(KV_WIDTH), tok(KV_WIDTH), slab, slab, slab],
        out_shape=[tok_shape(ATTN_WIDTH), tok_shape(KV_WIDTH), tok_shape(KV_WIDTH), slab_shape, slab_shape, slab_shape],
        scratch_shapes=[pltpu.VMEM((BG * tt, D_MODEL), BF16)],
        compiler_params=_cparams(("arbitrary", "arbitrary")),
        name="mixer_in_proj",
    )(*args)


def _attn_kernel(q_ref, k_ref, v_ref, *rest, n_ctx):
    if n_ctx:
        kc_ref, vc_ref, o_ref, kd_ref, va_ref, vb_ref = rest
    else:
        o_ref, kd_ref, va_ref, vb_ref = rest
    lk = k_ref.shape[0]
    tq = q_ref.shape[0]

    def lane_lt(rows):
        return lax.broadcasted_iota(jnp.int32, (rows, LANES), 1) < HEAD_DIM

    @pl.when(pl.program_id(1) == 0)
    def _():
        def fill(src_k, src_v, base, rows):
            low = lane_lt(rows)
            for pair in range(N_KV_HEADS // 2):
                sl = slice(pair * LANES, (pair + 1) * LANES)
                k2 = src_k[:, sl]
                v2 = src_v[:, sl]
                k2r = pltpu.roll(k2, HEAD_DIM, 1)
                v2r = pltpu.roll(v2, HEAD_DIM, 1)
                for odd in range(2):
                    j = 2 * pair + odd
                    kd = jnp.where(low, k2r, k2) if odd else jnp.where(low, k2, k2r)
                    vd = jnp.where(low, v2r, v2) if odd else jnp.where(low, v2, v2r)
                    kd_ref[j, base:base + rows, :] = kd.astype(BF16)
                    va_ref[j, base:base + rows, :] = jnp.where(low, vd, 0.0).astype(BF16)
                    vb_ref[j, base:base + rows, :] = jnp.where(low, 0.0, vd).astype(BF16)

        if n_ctx:
            fill(kc_ref, vc_ref, 0, n_ctx)
        fill(k_ref, v_ref, n_ctx, lk)

    low = lane_lt(tq)
    for j in range(N_KV_HEADS):
        q2 = q_ref[:, j * LANES:(j + 1) * LANES]
        kd = kd_ref[j]
        s0 = _dot_nt(jnp.where(low, q2, 0.0).astype(BF16), kd)
        s1 = _dot_nt(jnp.where(low, 0.0, q2).astype(BF16), kd)
        p0 = jnp.exp(s0 - jnp.max(s0, axis=-1, keepdims=True))
        p1 = jnp.exp(s1 - jnp.max(s1, axis=-1, keepdims=True))
        r0 = 1.0 / jnp.sum(p0, axis=-1, keepdims=True)
        r1 = 1.0 / jnp.sum(p1, axis=-1, keepdims=True)
        o2 = _dot(p0.astype(BF16), va_ref[j]) + _dot(p1.astype(BF16), vb_ref[j])
        o_ref[:, j * LANES:(j + 1) * LANES] = o2 * jnp.where(low, r0, r1)


def _attention(q, k, v, ctx, tq):
    batch, seq, _ = q.shape
    n_ctx = 0 if ctx is None else ctx[0].shape[2]
    tok = lambda w: pl.BlockSpec((None, seq, w), lambda b, i: (b, 0, 0))
    in_specs = [pl.BlockSpec((None, tq, ATTN_WIDTH), lambda b, i: (b, i, 0)), tok(KV_WIDTH), tok(KV_WIDTH)]
    args = [q, k, v]
    if ctx is not None:
        cache_k, cache_v, layer = ctx
        cspec = pl.BlockSpec((None, None, n_ctx, KV_WIDTH), lambda b, i: (b, layer, 0, 0))
        in_specs += [cspec, cspec]
        args += [cache_k, cache_v]
    lt = n_ctx + seq
    return pl.pallas_call(
        functools.partial(_attn_kernel, n_ctx=n_ctx),
        grid=(batch, seq // tq),
        in_specs=in_specs,
        out_specs=pl.BlockSpec((None, tq, ATTN_WIDTH), lambda b, i: (b, i, 0)),
        out_shape=jax.ShapeDtypeStruct((batch, seq, ATTN_WIDTH), F32),
        scratch_shapes=[pltpu.VMEM((N_KV_HEADS, lt, LANES), BF16)] * 3,
        compiler_params=_cparams(("arbitrary", "arbitrary")),
        name="gqa_attention",
    )(*args)


def _rglru_kernel(rx_ref, rg_ref, cw_ref, cb_ref, wg_ref, bg_ref, lam_ref, h0_ref, y_ref, fin_ref,
                  xpad_ref, af_ref, uf_ref, ab_ref, ub_ref, *, seq, chunk):
    n = seq * BG
    pad = CONV_LEFT * BG
    n_chunks = n // chunk
    zeros = jnp.zeros((pad, LANES), F32)
    xpad_ref[0:pad, :] = zeros
    xpad_ref[pad + n:pad + n + pad, :] = zeros

    def copy_in(i, c):
        r0 = pl.multiple_of(i * chunk, chunk)
        xpad_ref[pl.ds(pad + r0, chunk), :] = rx_ref[pl.ds(r0, chunk), :]
        return c

    lax.fori_loop(0, n_chunks, copy_in, 0)

    neg = -lam_ref[...]
    softplus = jnp.maximum(neg, 0.0) + jnp.log1p(jnp.exp(-jnp.abs(neg)))
    decay = -LRU_C * softplus
    cw = cw_ref[...]
    cb = cb_ref[...]

    def gates(i, c):
        r0 = pl.multiple_of(i * chunk, chunk)
        xc = cb
        for j in range(CONV_W):
            xc = xc + xpad_ref[pl.ds(r0 + j * BG, chunk), :] * cw[j:j + 1, :]
        g = _dot(xc.astype(BF16), wg_ref[...]) + bg_ref[...]
        for d, (a_ref, u_ref) in enumerate(((af_ref, uf_ref), (ab_ref, ub_ref))):
            r = _sigmoid(g[:, (2 * d) * LANES:(2 * d + 1) * LANES])
            gi = _sigmoid(g[:, (2 * d + 1) * LANES:(2 * d + 2) * LANES])
            log_a = r * decay[d:d + 1, :]
            a_ref[pl.ds(r0, chunk), :] = jnp.exp(log_a)
            th = jnp.tanh(log_a)
            w = -2.0 * th
            mult = jnp.where(w > 0.0, w * lax.rsqrt(w * (1.0 - th)), 0.0)
            u_ref[pl.ds(r0, chunk), :] = mult * (gi * xc)
        return c

    lax.fori_loop(0, n_chunks, gates, 0)

    def step(t, carry):
        hf, hb = carry
        rf = pl.multiple_of(t * BG, BG)
        rb = pl.multiple_of((seq - 1 - t) * BG, BG)
        hf = af_ref[pl.ds(rf, BG), :] * hf + uf_ref[pl.ds(rf, BG), :]
        y_ref[pl.ds(rf, BG), :] = hf
        hb = ab_ref[pl.ds(rb, BG), :] * hb + ub_ref[pl.ds(rb, BG), :]
        xpad_ref[pl.ds(rb, BG), :] = hb
        return hf, hb

    hf, hb = lax.fori_loop(0, seq, step, (h0_ref[0], h0_ref[1]), unroll=8)
    fin_ref[0] = hf
    fin_ref[1] = hb

    def emit(i, c):
        r0 = pl.multiple_of(i * chunk, chunk)
        gate = jax.nn.gelu(rg_ref[pl.ds(r0, chunk), :])
        y_ref[pl.ds(r0, chunk), :] = gate * (y_ref[pl.ds(r0, chunk), :] + xpad_ref[pl.ds(r0, chunk), :])
        return c

    lax.fori_loop(0, n_chunks, emit, 0)


def _rglru(rx, rg, pp, l, h0, chunk=512):
    nbg, _, n, _ = rx.shape
    seq = n // BG
    slab = pl.BlockSpec((None, None, n, LANES), lambda g, c: (g, c, 0, 0))
    col = lambda rows: pl.BlockSpec((None, rows, LANES), lambda g, c: (l, 0, c))
    state = pl.BlockSpec((N_DIR, BG, LANES), lambda g, c: (0, g, c))
    return pl.pallas_call(
        functools.partial(_rglru_kernel, seq=seq, chunk=chunk),
        grid=(nbg, N_SLAB),
        in_specs=[slab, slab, col(CONV_W), col(1),
                  pl.BlockSpec((None, None, LANES, 4 * LANES), lambda g, c: (l, c, 0, 0)),
                  pl.BlockSpec((None, None, 1, 4 * LANES), lambda g, c: (l, c, 0, 0)),
                  col(N_DIR), state],
        out_specs=[slab, state],
        out_shape=[jax.ShapeDtypeStruct(rx.shape, F32), jax.ShapeDtypeStruct((N_DIR, nbg * BG, D_RNN), F32)],
        scratch_shapes=[pltpu.VMEM((n + 2 * CONV_LEFT * BG, LANES), F32)] + [pltpu.VMEM((n, LANES), F32)] * 4,
        compiler_params=_cparams(("arbitrary", "arbitrary")),
        name="rglru_scan",
    )(rx, rg, pp['conv_w'], pp['conv_b'], pp['wg'], pp['bg'], pp['lam'], h0)


def _s5_param_kernel(lr_ref, li_ref, ls_ref, br_ref, bi_ref, are_ref, aim_ref, bbr_ref, bbi_ref):
    lr = lr_ref[...]
    li = li_ref[...]
    step = jnp.exp(ls_ref[...])
    mag = jnp.exp(lr * step)
    ab_re = mag * jnp.cos(li * step)
    ab_im = mag * jnp.sin(li * step)
    den = lr * lr + li * li
    nr = ab_re - 1.0
    ni = ab_im
    f_re = (nr * lr + ni * li) / den
    f_im = (ni * lr - nr * li) / den
    are_ref[...] = ab_re
    aim_ref[...] = ab_im
    br = br_ref[...]
    bi = bi_ref[...]
    bbr_ref[...] = f_re[:, None, :] * br - f_im[:, None, :] * bi
    bbi_ref[...] = f_re[:, None, :] * bi + f_im[:, None, :] * br


def _s5_params(lam_re, lam_im, log_step, b_re, b_im):
    rows = DEPTH * N_DIR * SSM_GROUPS
    flat = lambda a: a.reshape(rows, SSM_STATE)
    bt = lambda a: a.reshape(rows, SSM_STATE, SSM_GROUP_W).transpose(0, 2, 1)
    vec = jax.ShapeDtypeStruct((rows, SSM_STATE), F32)
    mat = jax.ShapeDtypeStruct((rows, SSM_GROUP_W, SSM_STATE), F32)
    return pl.pallas_call(
        _s5_param_kernel,
        out_shape=[vec, vec, mat, mat],
        name="s5_discretise",
    )(flat(lam_re), flat(lam_im), log_step.reshape(rows, 1), bt(b_re), bt(b_im))


def _s5_kernel(uf_ref, ub_ref, bm_ref, cm_ref, a_ref, s0_ref, yf_ref, yb_ref, fin_ref, bu_ref, xs_ref, st_ref):
    rows = uf_ref.shape[1]
    steps = rows // BG
    i = pl.program_id(1)

    @pl.when(i == 0)
    def _():
        st_ref[...] = s0_ref[...]

    for d, u_ref in enumerate((uf_ref, ub_ref)):
        for s in range(SSM_SG):
            bu_ref[d, :, s * 2 * SG_STATE:(s + 1) * 2 * SG_STATE] = _dot(u_ref[s].astype(BF16), bm_ref[d, s])

    pair_cols = [(s * 2 * SG_STATE + j * LANES, s * 2 * SG_STATE + SG_STATE + j * LANES)
                 for s in range(SSM_SG) for j in range(SG_STATE // LANES)]
    for d in range(N_DIR):
        state = [(st_ref[d, :, cr:cr + LANES], st_ref[d, :, ci:ci + LANES]) for cr, ci in pair_cols]
        for t in (range(steps) if d == 0 else range(steps - 1, -1, -1)):
            r0 = t * BG
            for p, (cr, ci) in enumerate(pair_cols):
                xr, xi = state[p]
                ar = a_ref[d, :, cr:cr + LANES]
                ai = a_ref[d, :, ci:ci + LANES]
                nr = ar * xr - ai * xi + bu_ref[d, r0:r0 + BG, cr:cr + LANES]
                ni = ar * xi + ai * xr + bu_ref[d, r0:r0 + BG, ci:ci + LANES]
                xs_ref[d, r0:r0 + BG, cr:cr + LANES] = nr
                xs_ref[d, r0:r0 + BG, ci:ci + LANES] = ni
                state[p] = (nr, ni)
        for (cr, ci), (xr, xi) in zip(pair_cols, state):
            st_ref[d, :, cr:cr + LANES] = xr
            st_ref[d, :, ci:ci + LANES] = xi

    for d, y_ref in enumerate((yf_ref, yb_ref)):
        for s in range(SSM_SG):
            xs = xs_ref[d, :, s * 2 * SG_STATE:(s + 1) * 2 * SG_STATE].astype(BF16)
            y_ref[s] = _dot(xs, cm_ref[d, s])

    @pl.when(i == pl.num_programs(1) - 1)
    def _():
        fin_ref[...] = st_ref[...]


def _s5(su, pp, l, s0, rows=512):
    nbg, _, n, _ = su.shape
    nt = n // rows
    fwd = pl.BlockSpec((None, N_SLAB, rows, LANES), lambda g, i: (g, 0, i, 0))
    bwd = pl.BlockSpec((None, N_SLAB, rows, LANES), lambda g, i: (g, 0, nt - 1 - i, 0))
    st = pl.BlockSpec((N_DIR, BG, SSM_FLAT), lambda g, i: (0, g, 0))
    return pl.pallas_call(
        _s5_kernel,
        grid=(nbg, nt),
        in_specs=[fwd, bwd, _const((N_DIR, SSM_SG, LANES, 2 * SG_STATE), (l,), single=True),
                  _const((N_DIR, SSM_SG, 2 * SG_STATE, LANES), (l,), single=True),
                  _const((N_DIR, BG, SSM_FLAT), (l,)), st],
        out_specs=[fwd, bwd, st],
        out_shape=[jax.ShapeDtypeStruct(su.shape, F32)] * 2 + [jax.ShapeDtypeStruct((N_DIR, nbg * BG, SSM_FLAT), F32)],
        scratch_shapes=[pltpu.VMEM((N_DIR, rows, SSM_FLAT), F32), pltpu.VMEM((N_DIR, rows, SSM_FLAT), F32),
                        pltpu.VMEM((N_DIR, BG, SSM_FLAT), F32)],
        compiler_params=_cparams(("arbitrary", "arbitrary")),
        name="s5_scan",
    )(su, su, pp['bm'], pp['cm'], pp['a_flat'], s0)


def _route(logits):
    rows = logits.shape[0]
    lane = lax.broadcasted_iota(jnp.int32, (rows, ROUTE_W), 1)
    neg_inf = -jnp.inf
    big = ROUTE_W

    def row_max(v):
        return jnp.max(v, axis=-1, keepdims=True)

    def first_lane(mask):
        return jnp.min(jnp.where(mask, lane, big), axis=-1, keepdims=True)

    in_g = lane < N_EXPERT_GROUPS
    gl = jnp.where(in_g, logits, neg_inf)
    g_max = row_max(gl)
    g_exp = jnp.where(in_g, jnp.exp(gl - g_max), 0.0)
    g_w = 1.0 / jnp.sum(g_exp, axis=-1, keepdims=True)
    g_sel = first_lane(gl == g_max)
    e_lo = N_EXPERT_GROUPS + g_sel * EXPERTS_PER_GROUP
    in_e = (lane >= e_lo) & (lane < e_lo + EXPERTS_PER_GROUP)
    el = jnp.where(in_e, logits, neg_inf)
    v1 = row_max(el)
    i1 = first_lane(el == v1)
    el2 = jnp.where(lane == i1, neg_inf, el)
    v2 = row_max(el2)
    i2 = first_lane(el2 == v2)
    e2 = jnp.exp(v2 - v1)
    w1 = g_w / (1.0 + e2)
    w2 = g_w * e2 / (1.0 + e2)
    return jnp.where(lane == i1, w1, jnp.where(lane == i2, w2, 0.0))


def _merge_kernel(x_ref, o_ref, yr_ref, su_ref, yf_ref, yb_ref, mod_ref, g1_ref, g2_ref, *rest):
    gate_refs = rest[:N_GATE_BLK]
    (wa_ref, wr_ref, dsk_ref, wglu_ref, wout_ref, wrt_ref, brt_ref, x1_ref, h2_ref, comb_ref,
     h_scr, yr_scr, ys_scr, h2_scr) = rest[N_GATE_BLK:]
    tt = x_ref.shape[1]
    g1 = g1_ref[...]
    g2 = g2_ref[...]
    dsk = dsk_ref[...]
    per_gate = D_MODEL // GATE_BLK
    for b in range(BG):
        rb = slice(b * tt, (b + 1) * tt)
        h_scr[rb, :] = _modulate(x_ref[b], g1, mod_ref, b, 0, 1).astype(BF16)
        tm_rows = _strided_rows(b, tt)
        for c in range(N_SLAB):
            cs = slice(c * LANES, (c + 1) * LANES)
            yr_scr[rb, cs] = yr_ref.at[c][tm_rows, :].astype(BF16)
            ys = dsk[:, cs] * su_ref.at[c][tm_rows, :] + yf_ref.at[c][tm_rows, :] + yb_ref.at[c][tm_rows, :]
            ys_scr[rb, cs] = jax.nn.gelu(ys).astype(BF16)
    h = h_scr[...]
    y_attn = _dot(o_ref[...].reshape(BG * tt, ATTN_WIDTH).astype(BF16), wa_ref[...])
    y_rnn = _dot(yr_scr[...], wr_ref[...])
    glu = _dot(ys_scr[...], wglu_ref[...])
    y_ssm = glu[:, :D_MODEL] * _sigmoid(glu[:, D_MODEL:])
    mo = None
    for part in range(per_gate):
        cs = slice(part * GATE_BLK, (part + 1) * GATE_BLK)
        mix = None
        for j, y in enumerate((y_attn, y_rnn, y_ssm)):
            term = _sigmoid(_dot(h, gate_refs[j * per_gate + part][...])) * y[:, cs]
            mix = term if mix is None else mix + term
        contrib = _dot(mix.astype(BF16), wout_ref[cs, :])
        mo = contrib if mo is None else mo + contrib
    for b in range(BG):
        rb = slice(b * tt, (b + 1) * tt)
        x1 = x_ref[b] + mo[rb, :] * mod_ref[b, 2:3, :]
        x1_ref[b] = x1
        h2 = _modulate(x1, g2, mod_ref, b, 3, 4)
        h2_ref[b] = h2.astype(BF16)
        h2_scr[rb, :] = h2
    comb = _route(_dot3(h2_scr[...], wrt_ref[...]) + brt_ref[...])
    comb_ref[...] = comb.reshape(BG, tt, ROUTE_W)


def _merge(x, o, yr, su, yf, yb, mod, pp, l, tt=64):
    batch, seq, _ = x.shape
    nbg = batch // BG
    rows = BG * tt
    tok = lambda w: pl.BlockSpec((BG, tt, w), lambda g, i: (g, i, 0))
    slab = pl.BlockSpec((None, N_SLAB, rows, LANES), lambda g, i: (g, 0, i, 0))
    wspec = lambda *shape: _const(shape, (l,), single=True)
    gate_blk0 = PROJ_W // GATE_BLK
    gate = lambda j: pl.BlockSpec((None, D_MODEL, GATE_BLK), lambda *_: (l, 0, gate_blk0 + j),
                                  pipeline_mode=pl.Buffered(1))
    return pl.pallas_call(
        _merge_kernel,
        grid=(nbg, seq // tt),
        in_specs=[tok(D_MODEL), tok(ATTN_WIDTH), slab, slab, slab, slab,
                  _const((BG, N_MOD, D_MODEL)), _const((1, D_MODEL), (l,)), _const((1, D_MODEL), (l,))]
                 + [gate(j) for j in range(N_GATE_BLK)]
                 + [wspec(ATTN_WIDTH, D_MODEL), wspec(D_RNN, D_MODEL),
                  _const((1, D_SSM), (l,)), wspec(D_SSM, 2 * D_MODEL), wspec(D_MODEL, D_MODEL),
                  wspec(D_MODEL, ROUTE_W), _const((1, ROUTE_W), (l,))],
        out_specs=[tok(D_MODEL), tok(D_MODEL), tok(ROUTE_W)],
        out_shape=[jax.ShapeDtypeStruct((batch, seq, D_MODEL), F32), jax.ShapeDtypeStruct((batch, seq, D_MODEL), BF16),
                   jax.ShapeDtypeStruct((batch, seq, ROUTE_W), F32)],
        scratch_shapes=[pltpu.VMEM((rows, D_MODEL), BF16), pltpu.VMEM((rows, D_RNN), BF16),
                        pltpu.VMEM((rows, D_SSM), BF16), pltpu.VMEM((rows, D_MODEL), F32)],
        compiler_params=_cparams(("arbitrary", "arbitrary")),
        name="mixer_merge",
    )(x, o, yr, su, yf, yb, mod, pp['norm1'], pp['norm2'], *([pp['w_in']] * N_GATE_BLK), pp['w_attn_o'],
      pp['w_rnn_o'], pp['d_skip'], pp['w_glu'], pp['w_out'], pp['w_rt'], pp['b_rt'])


def _moe_kernel(x1_ref, h2_ref, comb_ref, mod_ref, wg_ref, wu_ref, wd_ref, *rest, final):
    if final:
        gf_ref, x2_ref, y_ref, acc_ref = rest
    else:
        x2_ref, acc_ref = rest
    step = pl.program_id(1)

    @pl.when(step == 0)
    def _():
        acc_ref[...] = jnp.zeros_like(acc_ref)

    comb = comb_ref[...]
    lane = lax.broadcasted_iota(jnp.int32, comb.shape, 1)
    h2 = h2_ref[...]
    total = None
    for j in range(EXPERTS_PER_STEP):
        e = step * EXPERTS_PER_STEP + j
        w_e = jnp.sum(jnp.where(lane == e + N_EXPERT_GROUPS, comb, 0.0), axis=-1, keepdims=True)
        gate = _dot(h2, wg_ref[j])
        act = gate * _sigmoid(gate) * _dot(h2, wu_ref[j]) * w_e
        out = _dot(act.astype(BF16), wd_ref[j])
        total = out if total is None else total + out
    acc_ref[...] += total

    @pl.when(step == N_EXPERTS // EXPERTS_PER_STEP - 1)
    def _():
        x2 = x1_ref[...] + acc_ref[...] * mod_ref[5:6, :]
        x2_ref[...] = x2
        if final:
            y_ref[...] = _rms(x2, gf_ref[...])


def _moe(x1, h2, comb, mod, pp, l, seq, g_final, tm=1024):
    n = x1.shape[0]
    final = g_final is not None
    per_batch = mod.shape[0] > 1
    assert not per_batch or seq % tm == 0
    mod_idx = (lambda i, e: (i * tm // seq, 0, 0)) if per_batch else (lambda i, e: (0, 0, 0))
    rows = lambda w: pl.BlockSpec((tm, w), lambda i, e: (i, 0))
    in_specs = [rows(D_MODEL), rows(D_MODEL), rows(ROUTE_W),
                pl.BlockSpec((None, N_MOD, D_MODEL), mod_idx),
                pl.BlockSpec((None, EXPERTS_PER_STEP, D_MODEL, D_EXPERT), lambda i, e: (l, e, 0, 0)),
                pl.BlockSpec((None, EXPERTS_PER_STEP, D_MODEL, D_EXPERT), lambda i, e: (l, e, 0, 0)),
                pl.BlockSpec((None, EXPERTS_PER_STEP, D_EXPERT, D_MODEL), lambda i, e: (l, e, 0, 0))]
    args = [x1, h2, comb, mod, pp['w_gate'], pp['w_up'], pp['w_down']]
    out_specs = [rows(D_MODEL)]
    out_shape = [jax.ShapeDtypeStruct((n, D_MODEL), F32)]
    if final:
        in_specs.append(pl.BlockSpec((1, D_MODEL), lambda i, e: (0, 0)))
        args.append(g_final)
        out_specs.append(rows(D_MODEL))
        out_shape.append(jax.ShapeDtypeStruct((n, D_MODEL), F32))
    return pl.pallas_call(
        functools.partial(_moe_kernel, final=final),
        grid=(n // tm, N_EXPERTS // EXPERTS_PER_STEP),
        in_specs=in_specs,
        out_specs=out_specs,
        out_shape=out_shape,
        scratch_shapes=[pltpu.VMEM((tm, D_MODEL), F32)],
        compiler_params=_cparams(("arbitrary", "arbitrary")),
        name="hier_moe",
    )(*args)


def _head_mean_matrix(width):
    idx = np.arange(width) // HEAD_DIM
    return jnp.asarray((idx[:, None] == idx[None, :]).astype(np.float32) / HEAD_DIM, dtype=BF16)


def _rope_tables(seq):
    t = np.arange(seq)
    row = (t // GRID_W).astype(np.float64)
    col = (t % GRID_W).astype(np.float64)
    inv = ROPE_BASE ** (-np.arange(ROPE_PAIRS_PER_AXIS, dtype=np.float64) / ROPE_PAIRS_PER_AXIS)
    inv = inv.astype(np.float32).astype(np.float64)
    ang = np.concatenate([row[:, None] * inv, col[:, None] * inv], axis=-1)
    ang = ang.astype(np.float32).astype(np.float64)
    cos = np.repeat(np.cos(ang), 2, axis=-1)
    sin = np.repeat(np.sin(ang), 2, axis=-1) * np.tile(np.array([-1.0, 1.0]), HEAD_DIM // 2)
    expand = lambda a: jnp.asarray(np.tile(a, (1, LANES // HEAD_DIM)), dtype=F32)
    return expand(cos), expand(sin)


def _prep_params(p, s5p):
    bf = lambda a: a.astype(BF16)
    pp = {}
    pp['w_in'] = bf(p['w_in'])
    pp['norm1'] = p['norm1'][:, None, :]
    pp['norm2'] = p['norm2'][:, None, :]
    pp['qg'] = jnp.tile(p['q_norm'], (1, N_HEADS))[:, None, :]
    pp['kg'] = jnp.tile(p['k_norm'], (1, N_KV_HEADS))[:, None, :]
    pp['s_q'] = _head_mean_matrix(ATTN_WIDTH)
    pp['s_k'] = _head_mean_matrix(KV_WIDTH)
    pp['w_attn_o'] = bf(p['w_attn_o'])
    pp['w_rnn_o'] = bf(p['w_rnn_o'])
    pp['w_glu'] = bf(p['w_glu'])
    pp['w_out'] = bf(p['w_out'])
    pp['d_skip'] = p['ssm_d'][:, None, :]
    pp['conv_w'] = p['conv_w']
    pp['conv_b'] = p['conv_b'][:, None, :]
    pp['lam'] = p['lru_lambda']
    per = LANES // RNN_BLOCK_W
    wa, wi = p['lru_w_a'], p['lru_w_i']
    w = jnp.stack([wa[:, 0], wi[:, 0], wa[:, 1], wi[:, 1]], axis=1)
    w = w.reshape(DEPTH, 4, N_SLAB, per, RNN_BLOCK_W, RNN_BLOCK_W)
    eye = jnp.eye(per, dtype=F32)
    wg = w.transpose(0, 2, 3, 4, 1, 5)[:, :, :, :, :, None, :] * eye[None, None, :, None, None, :, None]
    pp['wg'] = bf(wg.reshape(DEPTH, N_SLAB, LANES, 4 * LANES))
    ba, bi = p['lru_b_a'], p['lru_b_i']
    b = jnp.stack([ba[:, 0], bi[:, 0], ba[:, 1], bi[:, 1]], axis=1).reshape(DEPTH, 4, N_SLAB, LANES)
    pp['bg'] = b.transpose(0, 2, 1, 3).reshape(DEPTH, N_SLAB, 1, 4 * LANES)
    ab_re, ab_im, bb_re, bb_im = s5p
    eye_g = jnp.eye(SG_GROUPS, dtype=F32)
    lead = (DEPTH, N_DIR, SSM_SG, SG_GROUPS)
    bb = jnp.stack([bb_re, bb_im], axis=2).reshape(lead + (SSM_GROUP_W, 2, SSM_STATE))
    eye_g = bf(eye_g)
    bm = bf(bb)[:, :, :, :, :, :, None, :] * eye_g[None, None, None, :, None, None, :, None]
    pp['bm'] = bm.reshape(DEPTH, N_DIR, SSM_SG, LANES, 2 * SG_STATE)
    cc = jnp.stack([p['ssm_c_re'], -p['ssm_c_im']], axis=2)
    cc = cc.reshape(DEPTH, N_DIR, 2, SSM_SG, SG_GROUPS, SSM_GROUP_W, SSM_STATE).transpose(0, 1, 3, 2, 4, 6, 5)
    cm = bf(cc)[:, :, :, :, :, :, None, :] * eye_g[None, None, None, None, :, None, :, None]
    pp['cm'] = cm.reshape(DEPTH, N_DIR, SSM_SG, 2 * SG_STATE, LANES)
    are = ab_re.reshape(DEPTH, N_DIR, SSM_SG, SG_STATE)
    aim = ab_im.reshape(DEPTH, N_DIR, SSM_SG, SG_STATE)
    a_flat = jnp.concatenate([are, aim], axis=-1).reshape(DEPTH, N_DIR, 1, SSM_FLAT)
    pp['a_flat'] = jnp.broadcast_to(a_flat, (DEPTH, N_DIR, BG, SSM_FLAT))
    fill = ROUTE_W - N_EXPERT_GROUPS - N_EXPERTS
    pp['w_rt'] = jnp.concatenate([p['router_g_w'], p['router_e_w'], jnp.zeros((DEPTH, D_MODEL, fill), F32)], axis=-1)
    pp['b_rt'] = jnp.concatenate([p['router_g_b'], p['router_e_b'], jnp.zeros((DEPTH, fill), F32)], axis=-1)[:, None, :]
    pp['w_gate'] = bf(p['w_e_gate'])
    pp['w_up'] = bf(p['w_e_up'])
    pp['w_down'] = bf(p['w_e_down'])
    return pp


def _flat_state(re, im):
    b = re.shape[0]
    r = re.transpose(1, 2, 0, 3, 4).reshape(DEPTH, N_DIR, b, SSM_SG, SG_STATE)
    i = im.transpose(1, 2, 0, 3, 4).reshape(DEPTH, N_DIR, b, SSM_SG, SG_STATE)
    return jnp.concatenate([r, i], axis=-1).reshape(DEPTH, N_DIR, b, SSM_FLAT)


def _unflat_state(flat):
    b = flat.shape[2]
    f = flat.reshape(DEPTH, N_DIR, b, SSM_SG, 2, SG_GROUPS, SSM_STATE).transpose(4, 2, 0, 1, 3, 5, 6)
    f = f.reshape(2, b, DEPTH, N_DIR, SSM_GROUPS, SSM_STATE)
    return f[0], f[1]


def _trunk_layer(x, mod, mod_moe, pp, l, ctx, rope_tabs, g_final, attn_tq):
    batch, seq, _ = x.shape
    q, k, v, rx, rg, su = _in_proj(x, mod, pp, l, rope_tabs)
    if ctx is None:
        attn_ctx = None
        h0 = jnp.zeros((N_DIR, batch, D_RNN), F32)
        s0 = jnp.zeros((N_DIR, batch, SSM_FLAT), F32)
    else:
        cache_k, cache_v, h0, s0 = ctx
        attn_ctx = (cache_k, cache_v, l)
    o = _attention(q, k, v, attn_ctx, attn_tq)
    yr, rnn_fin = _rglru(rx, rg, pp, l, h0)
    yf, yb, ssm_fin = _s5(su, pp, l, s0)
    x1, h2, comb = _merge(x, o, yr, su, yf, yb, mod, pp, l)
    flat = lambda a: a.reshape(batch * seq, a.shape[-1])
    outs = _moe(flat(x1), flat(h2), flat(comb), mod_moe, pp, l, seq, g_final)
    return [a.reshape(batch, seq, D_MODEL) for a in outs], (k, v, rnn_fin, ssm_fin)


def kernel(x_prompt, x_sample, cache_k, cache_v, state_rglru, state_ssm_re, state_ssm_im, c, c_ctx, w_mod, b_mod, norm1, norm2, w_in, q_norm, k_norm, w_attn_o, conv_w, conv_b, lru_w_a, lru_b_a, lru_w_i, lru_b_i, lru_lambda, w_rnn_o, ssm_lam_re, ssm_lam_im, ssm_log_step, ssm_b_re, ssm_b_im, ssm_c_re, ssm_c_im, ssm_d, w_glu, w_out, router_g_w, router_g_b, router_e_w, router_e_b, w_e_gate, w_e_up, w_e_down, final_norm):
    p = dict(norm1=norm1, norm2=norm2, w_in=w_in, q_norm=q_norm, k_norm=k_norm, w_attn_o=w_attn_o, conv_w=conv_w,
             conv_b=conv_b, lru_w_a=lru_w_a, lru_b_a=lru_b_a, lru_w_i=lru_w_i, lru_b_i=lru_b_i,
             lru_lambda=lru_lambda, w_rnn_o=w_rnn_o, ssm_c_re=ssm_c_re, ssm_c_im=ssm_c_im, ssm_d=ssm_d, w_glu=w_glu,
             w_out=w_out, router_g_w=router_g_w, router_g_b=router_g_b, router_e_w=router_e_w,
             router_e_b=router_e_b, w_e_gate=w_e_gate, w_e_up=w_e_up, w_e_down=w_e_down)
    bc, lc, _ = x_prompt.shape
    bd, ld, _ = x_sample.shape
    past = cache_k.shape[2]
    assert bd == BG and bc % BG == 0

    n_cond = 1 + bd
    cond_rows = -(-n_cond // SUBLANES) * SUBLANES
    cond = jnp.concatenate([c_ctx[None], c, jnp.zeros((cond_rows - n_cond, D_MODEL), F32)], axis=0)
    mods = _modulation(cond, w_mod, b_mod).reshape(DEPTH, cond_rows, N_MOD, D_MODEL)
    s5p = _s5_params(ssm_lam_re, ssm_lam_im, ssm_log_step, ssm_b_re, ssm_b_im)
    pp = _prep_params(p, s5p)
    rope_tabs = _rope_tables(ld)
    final_g = final_norm[None]

    ck = cache_k.reshape(bd, DEPTH, past, KV_WIDTH)
    cv = cache_v.reshape(bd, DEPTH, past, KV_WIDTH)
    h0_lat = state_rglru.transpose(1, 2, 0, 3)
    s0_lat = _flat_state(state_ssm_re, state_ssm_im)
    x_ctx, x_lat = x_prompt, x_sample
    ks, vs, rs, ss = [], [], [], []
    y_ctx = y_lat = None
    for l in range(DEPTH):
        g_final = final_g if l == DEPTH - 1 else None
        mod_ctx1 = mods[l, 0:1]
        mod_ctx = jnp.broadcast_to(mod_ctx1, (BG, N_MOD, D_MODEL))
        mod_lat = mods[l, 1:n_cond]
        outs, (k_l, v_l, r_l, s_l) = _trunk_layer(x_ctx, mod_ctx, mod_ctx1, pp, l, None, None, g_final, lc)
        x_ctx = outs[0]
        if g_final is not None:
            y_ctx = outs[1]
        ks.append(k_l)
        vs.append(v_l)
        rs.append(r_l)
        ss.append(s_l)
        ctx = (ck, cv, h0_lat[l], s0_lat[l])
        outs, _ = _trunk_layer(x_lat, mod_lat, mod_lat, pp, l, ctx, rope_tabs, g_final, 512)
        x_lat = outs[0]
        if g_final is not None:
            y_lat = outs[1]
    new_k = jnp.stack(ks, axis=1).reshape(bc, DEPTH, lc, N_KV_HEADS, HEAD_DIM)
    new_v = jnp.stack(vs, axis=1).reshape(bc, DEPTH, lc, N_KV_HEADS, HEAD_DIM)
    new_r = jnp.stack(rs, axis=0).transpose(2, 0, 1, 3)
    new_re, new_im = _unflat_state(jnp.stack(ss, axis=0))
    return (y_ctx, y_lat, new_k, new_v, new_r, new_re, new_im)
```

```python
import functools

import numpy as np
import jax
import jax.numpy as jnp
from jax import lax
from jax.experimental import pallas as pl
from jax.experimental.pallas import tpu as pltpu

F32 = jnp.float32
BF16 = jnp.bfloat16

D_MODEL = 1024
DEPTH = 2
GRID_W = 64
N_DIR = 2
NORM_EPS = 1e-6
N_MOD = 6

N_HEADS = 8
N_KV_HEADS = 4
HEAD_DIM = 64
ATTN_WIDTH = N_HEADS * HEAD_DIM
KV_WIDTH = N_KV_HEADS * HEAD_DIM
ROPE_BASE = 10000.0
ROPE_PAIRS_PER_AXIS = HEAD_DIM // 4

D_RNN = D_MODEL // 2
RNN_BLOCKS = 8
RNN_BLOCK_W = D_RNN // RNN_BLOCKS
CONV_W = 4
CONV_LEFT = 2
LRU_C = 8.0

D_SSM = D_MODEL // 2
SSM_GROUP_W = 16
SSM_GROUPS = D_SSM // SSM_GROUP_W
SSM_STATE = 64

N_EXPERT_GROUPS = 4
EXPERTS_PER_GROUP = 4
N_EXPERTS = N_EXPERT_GROUPS * EXPERTS_PER_GROUP
D_EXPERT = D_MODEL // 4

LANES = 128
SUBLANES = 8
VMEM_LIMIT = 56 * 1024 * 1024

BG = SUBLANES
PROJ_W = ATTN_WIDTH + 2 * KV_WIDTH + 2 * D_RNN + D_SSM
N_SLAB = D_RNN // LANES
GATE_BLK = 512
N_GATE_BLK = 3 * D_MODEL // GATE_BLK
assert PROJ_W % GATE_BLK == 0

SG_GROUPS = LANES // SSM_GROUP_W
SSM_SG = SSM_GROUPS // SG_GROUPS
SG_STATE = SG_GROUPS * SSM_STATE
SSM_FLAT = SSM_SG * 2 * SG_STATE

ROUTE_W = LANES
GID_LANE = ROUTE_W - 1
GROUP_W = EXPERTS_PER_GROUP * D_EXPERT
MOE_T = 512
MOE_W = 192
ROW_ALIGN = 16
MOE_TP = 768
assert MOE_TP >= MOE_T + MOE_W + ROW_ALIGN and MOE_TP % LANES == 0 and MOE_W % ROW_ALIGN == 0


def _cparams(sem, vmem=VMEM_LIMIT):
    return pltpu.CompilerParams(dimension_semantics=sem, vmem_limit_bytes=vmem)


def _const(shape, lead=(), single=False):
    idx = tuple(lead) + (0,) * len(shape)
    block = (None,) * len(lead) + tuple(shape)
    if single:
        return pl.BlockSpec(block, lambda *_: idx, pipeline_mode=pl.Buffered(1))
    return pl.BlockSpec(block, lambda *_: idx)


def _dot(a, b):
    return jnp.dot(a, b, preferred_element_type=F32)


def _dot_nt(a, b):
    return lax.dot_general(a, b, (((1,), (1,)), ((), ())), preferred_element_type=F32)


def _dot_tn(a, b):
    return lax.dot_general(a, b, (((0,), (0,)), ((), ())), preferred_element_type=F32)


def _split_bf16(a):
    hi = a.astype(BF16)
    lo = (a - hi.astype(F32)).astype(BF16)
    return hi, lo


def _dot3(a, b):
    a_hi, a_lo = _split_bf16(a)
    b_hi, b_lo = _split_bf16(b)
    return _dot(a_hi, b_hi) + (_dot(a_lo, b_hi) + _dot(a_hi, b_lo))


def _sigmoid(x):
    return 0.5 * jnp.tanh(0.5 * x) + 0.5


def _rms(x, g):
    ms = jnp.mean(x * x, axis=-1, keepdims=True)
    return x * lax.rsqrt(ms + NORM_EPS) * g


def _modulate(x, g, mod_ref, b, shift_i, scale_i):
    return _rms(x, g) * (1.0 + mod_ref[b, scale_i:scale_i + 1, :]) + mod_ref[b, shift_i:shift_i + 1, :]


def _strided_rows(b, n):
    return pl.ds(b, n, stride=BG)


def _mod_kernel(c_ref, w_ref, b_ref, o_ref):
    c = c_ref[...]
    a = c * _sigmoid(c)
    o_ref[...] = _dot3(a, w_ref[...]) + b_ref[...]


def _modulation(cond, w_mod, b_mod):
    rows = cond.shape[0]
    tn = 1536
    width = N_MOD * D_MODEL
    return pl.pallas_call(
        _mod_kernel,
        grid=(DEPTH, width // tn),
        in_specs=[
            pl.BlockSpec((rows, D_MODEL), lambda l, j: (0, 0)),
            pl.BlockSpec((None, D_MODEL, tn), lambda l, j: (l, 0, j)),
            pl.BlockSpec((None, 1, tn), lambda l, j: (l, 0, j)),
        ],
        out_specs=pl.BlockSpec((None, rows, tn), lambda l, j: (l, 0, j)),
        out_shape=jax.ShapeDtypeStruct((DEPTH, rows, width), F32),
        compiler_params=_cparams(("arbitrary", "arbitrary")),
        name="adaln_modulation",
    )(cond, w_mod, b_mod.reshape(DEPTH, 1, width))


def _head_norm(z, s_ref, gain):
    ms = _dot((z * z).astype(BF16), s_ref[...])
    return z * lax.rsqrt(ms + NORM_EPS) * gain


def _rope(z, cos, sin):
    rows = z.shape[0]
    tt = cos.shape[0]
    lane = lax.broadcasted_iota(jnp.int32, (rows, LANES), 1)
    partner = jnp.where((lane & 1) == 0, pltpu.roll(z, LANES - 1, 1), pltpu.roll(z, 1, 1))
    z3 = z.reshape(rows // tt, tt, LANES)
    p3 = partner.reshape(rows // tt, tt, LANES)
    return z3 * cos[None] + p3 * sin[None]


def _in_proj_kernel(x_ref, mod_ref, g_ref, w_ref, qg_ref, kg_ref, sq_ref, sk_ref, *rest, rope):
    if rope:
        cos_ref, sin_ref, q_ref, k_ref, v_ref, rx_ref, rg_ref, su_ref, h_scr = rest
    else:
        q_ref, k_ref, v_ref, rx_ref, rg_ref, su_ref, h_scr = rest
    tt = x_ref.shape[1]
    g = g_ref[...]
    qscale = HEAD_DIM ** -0.5
    for b in range(BG):
        h_scr[b * tt:(b + 1) * tt, :] = _modulate(x_ref[b], g, mod_ref, b, 0, 1).astype(BF16)
    h = h_scr[...]

    def proj(lo, width):
        return _dot(h, w_ref[:, lo:lo + width])

    q = _head_norm(proj(0, ATTN_WIDTH), sq_ref, qg_ref[...]) * qscale
    k = _head_norm(proj(ATTN_WIDTH, KV_WIDTH), sk_ref, kg_ref[...])
    if rope:
        cos = cos_ref[...]
        sin = sin_ref[...]
        for dst, z in ((q_ref, q), (k_ref, k)):
            for c in range(z.shape[1] // LANES):
                cs = slice(c * LANES, (c + 1) * LANES)
                dst[:, :, cs] = _rope(z[:, cs], cos, sin)
    else:
        q_ref[...] = q.reshape(BG, tt, ATTN_WIDTH)
        k_ref[...] = k.reshape(BG, tt, KV_WIDTH)
    off = ATTN_WIDTH + KV_WIDTH
    v_ref[...] = proj(off, KV_WIDTH).reshape(BG, tt, KV_WIDTH)
    off += KV_WIDTH
    for dst in (rx_ref, rg_ref, su_ref):
        z = proj(off, D_RNN)
        off += D_RNN
        for b in range(BG):
            for c in range(N_SLAB):
                dst.at[c][_strided_rows(b, tt), :] = z[b * tt:(b + 1) * tt, c * LANES:(c + 1) * LANES]


def _in_proj(x, mod, pp, l, rope_tabs, tt=64):
    batch, seq, _ = x.shape
    nbg = batch // BG
    rope = rope_tabs is not None
    tok = lambda w: pl.BlockSpec((BG, tt, w), lambda g, i: (g, i, 0))
    slab = pl.BlockSpec((None, N_SLAB, tt * BG, LANES), lambda g, i: (g, 0, i, 0))
    in_specs = [tok(D_MODEL), _const((BG, N_MOD, D_MODEL)), _const((1, D_MODEL), (l,)),
                pl.BlockSpec((None, D_MODEL, PROJ_W), lambda *_: (l, 0, 0), pipeline_mode=pl.Buffered(1)),
                _const((1, ATTN_WIDTH), (l,)),
                _const((1, KV_WIDTH), (l,)), _const((ATTN_WIDTH, ATTN_WIDTH)), _const((KV_WIDTH, KV_WIDTH))]
    args = [x, mod, pp['norm1'], pp['w_in'], pp['qg'], pp['kg'], pp['s_q'], pp['s_k']]
    if rope:
        tab = pl.BlockSpec((tt, LANES), lambda g, i: (i, 0))
        in_specs += [tab, tab]
        args += list(rope_tabs)
    tok_shape = lambda w: jax.ShapeDtypeStruct((batch, seq, w), F32)
    slab_shape = jax.ShapeDtypeStruct((nbg, N_SLAB, seq * BG, LANES), F32)
    return pl.pallas_call(
        functools.partial(_in_proj_kernel, rope=rope),
        grid=(nbg, seq // tt),
        in_specs=in_specs,
        out_specs=[tok(ATTN_WIDTH), tok(KV_WIDTH), tok(KV_WIDTH), slab, slab, slab],
        out_shape=[tok_shape(ATTN_WIDTH), tok_shape(KV_WIDTH), tok_shape(KV_WIDTH), slab_shape, slab_shape, slab_shape],
        scratch_shapes=[pltpu.VMEM((BG * tt, D_MODEL), BF16)],
        compiler_params=_cparams(("arbitrary", "arbitrary")),
        name="mixer_in_proj",
    )(*args)


def _attn_kernel(q_ref, k_ref, v_ref, *rest, n_ctx):
    if n_ctx:
        kc_ref, vc_ref, o_ref, kd_ref, va_ref, vb_ref = rest
    else:
        o_ref, kd_ref, va_ref, vb_ref = rest
    lk = k_ref.shape[0]
    tq = q_ref.shape[0]

    def lane_lt(rows):
        return lax.broadcasted_iota(jnp.int32, (rows, LANES), 1) < HEAD_DIM

    @pl.when(pl.program_id(1) == 0)
    def _():
        def fill(src_k, src_v, base, rows):
            low = lane_lt(rows)
            for pair in range(N_KV_HEADS // 2):
                sl = slice(pair * LANES, (pair + 1) * LANES)
                k2 = src_k[:, sl]
                v2 = src_v[:, sl]
                k2r = pltpu.roll(k2, HEAD_DIM, 1)
                v2r = pltpu.roll(v2, HEAD_DIM, 1)
                for odd in range(2):
                    j = 2 * pair + odd
                    kd = jnp.where(low, k2r, k2) if odd else jnp.where(low, k2, k2r)
                    vd = jnp.where(low, v2r, v2) if odd else jnp.where(low, v2, v2r)
                    kd_ref[j, base:base + rows, :] = kd.astype(BF16)
                    va_ref[j, base:base + rows, :] = jnp.where(low, vd, 0.0).astype(BF16)
                    vb_ref[j, base:base + rows, :] = jnp.where(low, 0.0, vd).astype(BF16)

        if n_ctx:
            fill(kc_ref, vc_ref, 0, n_ctx)
        fill(k_ref, v_ref, n_ctx, lk)

    low = lane_lt(tq)
    for j in range(N_KV_HEADS):
        q2 = q_ref[:, j * LANES:(j + 1) * LANES]
        kd = kd_ref[j]
        s0 = _dot_nt(jnp.where(low, q2, 0.0).astype(BF16), kd)
        s1 = _dot_nt(jnp.where(low, 0.0, q2).astype(BF16), kd)
        p0 = jnp.exp(s0 - jnp.max(s0, axis=-1, keepdims=True))
        p1 = jnp.exp(s1 - jnp.max(s1, axis=-1, keepdims=True))
        r0 = 1.0 / jnp.sum(p0, axis=-1, keepdims=True)
        r1 = 1.0 / jnp.sum(p1, axis=-1, keepdims=True)
        o2 = _dot(p0.astype(BF16), va_ref[j]) + _dot(p1.astype(BF16), vb_ref[j])
        o_ref[:, j * LANES:(j + 1) * LANES] = o2 * jnp.where(low, r0, r1)


def _attention(q, k, v, ctx, tq):
    batch, seq, _ = q.shape
    n_ctx = 0 if ctx is None else ctx[0].shape[2]
    tok = lambda w: pl.BlockSpec((None, seq, w), lambda b, i: (b, 0, 0))
    in_specs = [pl.BlockSpec((None, tq, ATTN_WIDTH), lambda b, i: (b, i, 0)), tok(KV_WIDTH), tok(KV_WIDTH)]
    args = [q, k, v]
    if ctx is not None:
        cache_k, cache_v, layer = ctx
        cspec = pl.BlockSpec((None, None, n_ctx, KV_WIDTH), lambda b, i: (b, layer, 0, 0))
        in_specs += [cspec, cspec]
        args += [cache_k, cache_v]
    lt = n_ctx + seq
    return pl.pallas_call(
        functools.partial(_attn_kernel, n_ctx=n_ctx),
        grid=(batch, seq // tq),
        in_specs=in_specs,
        out_specs=pl.BlockSpec((None, tq, ATTN_WIDTH), lambda b, i: (b, i, 0)),
        out_shape=jax.ShapeDtypeStruct((batch, seq, ATTN_WIDTH), F32),
        scratch_shapes=[pltpu.VMEM((N_KV_HEADS, lt, LANES), BF16)] * 3,
        compiler_params=_cparams(("arbitrary", "arbitrary")),
        name="gqa_attention",
    )(*args)


def _rglru_kernel(rx_ref, rg_ref, cw_ref, cb_ref, wg_ref, bg_ref, lam_ref, h0_ref, y_ref, fin_ref,
                  xpad_ref, af_ref, uf_ref, ab_ref, ub_ref, *, seq, chunk):
    n = seq * BG
    pad = CONV_LEFT * BG
    n_chunks = n // chunk
    zeros = jnp.zeros((pad, LANES), F32)
    xpad_ref[0:pad, :] = zeros
    xpad_ref[pad + n:pad + n + pad, :] = zeros

    def copy_in(i, c):
        r0 = pl.multiple_of(i * chunk, chunk)
        xpad_ref[pl.ds(pad + r0, chunk), :] = rx_ref[pl.ds(r0, chunk), :]
        return c

    lax.fori_loop(0, n_chunks, copy_in, 0)

    neg = -lam_ref[...]
    softplus = jnp.maximum(neg, 0.0) + jnp.log1p(jnp.exp(-jnp.abs(neg)))
    decay = -LRU_C * softplus
    cw = cw_ref[...]
    cb = cb_ref[...]

    def gates(i, c):
        r0 = pl.multiple_of(i * chunk, chunk)
        xc = cb
        for j in range(CONV_W):
            xc = xc + xpad_ref[pl.ds(r0 + j * BG, chunk), :] * cw[j:j + 1, :]
        g = _dot(xc.astype(BF16), wg_ref[...]) + bg_ref[...]
        for d, (a_ref, u_ref) in enumerate(((af_ref, uf_ref), (ab_ref, ub_ref))):
            r = _sigmoid(g[:, (2 * d) * LANES:(2 * d + 1) * LANES])
            gi = _sigmoid(g[:, (2 * d + 1) * LANES:(2 * d + 2) * LANES])
            log_a = r * decay[d:d + 1, :]
            a_ref[pl.ds(r0, chunk), :] = jnp.exp(log_a)
            th = jnp.tanh(log_a)
            w = -2.0 * th
            mult = jnp.where(w > 0.0, w * lax.rsqrt(w * (1.0 - th)), 0.0)
            u_ref[pl.ds(r0, chunk), :] = mult * (gi * xc)
        return c

    lax.fori_loop(0, n_chunks, gates, 0)

    def step(t, carry):
        hf, hb = carry
        rf = pl.multiple_of(t * BG, BG)
        rb = pl.multiple_of((seq - 1 - t) * BG, BG)
        hf = af_ref[pl.ds(rf, BG), :] * hf + uf_ref[pl.ds(rf, BG), :]
        y_ref[pl.ds(rf, BG), :] = hf
        hb = ab_ref[pl.ds(rb, BG), :] * hb + ub_ref[pl.ds(rb, BG), :]
        xpad_ref[pl.ds(rb, BG), :] = hb
        return hf, hb

    hf, hb = lax.fori_loop(0, seq, step, (h0_ref[0], h0_ref[1]), unroll=8)
    fin_ref[0] = hf
    fin_ref[1] = hb

    def emit(i, c):
        r0 = pl.multiple_of(i * chunk, chunk)
        gate = jax.nn.gelu(rg_ref[pl.ds(r0, chunk), :])
        y_ref[pl.ds(r0, chunk), :] = gate * (y_ref[pl.ds(r0, chunk), :] + xpad_ref[pl.ds(r0, chunk), :])
        return c

    lax.fori_loop(0, n_chunks, emit, 0)


def _rglru(rx, rg, pp, l, h0, chunk=512):
    nbg, _, n, _ = rx.shape
    seq = n // BG
    slab = pl.BlockSpec((None, None, n, LANES), lambda g, c: (g, c, 0, 0))
    col = lambda rows: pl.BlockSpec((None, rows, LANES), lambda g, c: (l, 0, c))
    state = pl.BlockSpec((N_DIR, BG, LANES), lambda g, c: (0, g, c))
    return pl.pallas_call(
        functools.partial(_rglru_kernel, seq=seq, chunk=chunk),
        grid=(nbg, N_SLAB),
        in_specs=[slab, slab, col(CONV_W), col(1),
                  pl.BlockSpec((None, None, LANES, 4 * LANES), lambda g, c: (l, c, 0, 0)),
                  pl.BlockSpec((None, None, 1, 4 * LANES), lambda g, c: (l, c, 0, 0)),
                  col(N_DIR), state],
        out_specs=[slab, state],
        out_shape=[jax.ShapeDtypeStruct(rx.shape, F32), jax.ShapeDtypeStruct((N_DIR, nbg * BG, D_RNN), F32)],
        scratch_shapes=[pltpu.VMEM((n + 2 * CONV_LEFT * BG, LANES), F32)] + [pltpu.VMEM((n, LANES), F32)] * 4,
        compiler_params=_cparams(("arbitrary", "arbitrary")),
        name="rglru_scan",
    )(rx, rg, pp['conv_w'], pp['conv_b'], pp['wg'], pp['bg'], pp['lam'], h0)


def _s5_param_kernel(lr_ref, li_ref, ls_ref, br_ref, bi_ref, are_ref, aim_ref, bbr_ref, bbi_ref):
    lr = lr_ref[...]
    li = li_ref[...]
    step = jnp.exp(ls_ref[...])
    mag = jnp.exp(lr * step)
    ab_re = mag * jnp.cos(li * step)
    ab_im = mag * jnp.sin(li * step)
    den = lr * lr + li * li
    nr = ab_re - 1.0
    ni = ab_im
    f_re = (nr * lr + ni * li) / den
    f_im = (ni * lr - nr * li) / den
    are_ref[...] = ab_re
    aim_ref[...] = ab_im
    br = br_ref[...]
    bi = bi_ref[...]
    bbr_ref[...] = f_re[:, None, :] * br - f_im[:, None, :] * bi
    bbi_ref[...] = f_re[:, None, :] * bi + f_im[:, None, :] * br


def _s5_params(lam_re, lam_im, log_step, b_re, b_im):
    rows = DEPTH * N_DIR * SSM_GROUPS
    flat = lambda a: a.reshape(rows, SSM_STATE)
    bt = lambda a: a.reshape(rows, SSM_STATE, SSM_GROUP_W).transpose(0, 2, 1)
    vec = jax.ShapeDtypeStruct((rows, SSM_STATE), F32)
    mat = jax.ShapeDtypeStruct((rows, SSM_GROUP_W, SSM_STATE), F32)
    return pl.pallas_call(
        _s5_param_kernel,
        out_shape=[vec, vec, mat, mat],
        name="s5_discretise",
    )(flat(lam_re), flat(lam_im), log_step.reshape(rows, 1), bt(b_re), bt(b_im))


def _s5_kernel(uf_ref, ub_ref, bm_ref, cm_ref, a_ref, s0_ref, yf_ref, yb_ref, fin_ref, bu_ref, xs_ref, st_ref):
    rows = uf_ref.shape[1]
    steps = rows // BG
    i = pl.program_id(1)

    @pl.when(i == 0)
    def _():
        st_ref[...] = s0_ref[...]

    for d, u_ref in enumerate((uf_ref, ub_ref)):
        for s in range(SSM_SG):
            bu_ref[d, :, s * 2 * SG_STATE:(s + 1) * 2 * SG_STATE] = _dot(u_ref[s].astype(BF16), bm_ref[d, s])

    pair_cols = [(s * 2 * SG_STATE + j * LANES, s * 2 * SG_STATE + SG_STATE + j * LANES)
                 for s in range(SSM_SG) for j in range(SG_STATE // LANES)]
    for d in range(N_DIR):
        state = [(st_ref[d, :, cr:cr + LANES], st_ref[d, :, ci:ci + LANES]) for cr, ci in pair_cols]
        for t in (range(steps) if d == 0 else range(steps - 1, -1, -1)):
            r0 = t * BG
            for p, (cr, ci) in enumerate(pair_cols):
                xr, xi = state[p]
                ar = a_ref[d, :, cr:cr + LANES]
                ai = a_ref[d, :, ci:ci + LANES]
                nr = ar * xr - ai * xi + bu_ref[d, r0:r0 + BG, cr:cr + LANES]
                ni = ar * xi + ai * xr + bu_ref[d, r0:r0 + BG, ci:ci + LANES]
                xs_ref[d, r0:r0 + BG, cr:cr + LANES] = nr
                xs_ref[d, r0:r0 + BG, ci:ci + LANES] = ni
                state[p] = (nr, ni)
        for (cr, ci), (xr, xi) in zip(pair_cols, state):
            st_ref[d, :, cr:cr + LANES] = xr
            st_ref[d, :, ci:ci + LANES] = xi

    for d, y_ref in enumerate((yf_ref, yb_ref)):
        for s in range(SSM_SG):
            xs = xs_ref[d, :, s * 2 * SG_STATE:(s + 1) * 2 * SG_STATE].astype(BF16)
            y_ref[s] = _dot(xs, cm_ref[d, s])

    @pl.when(i == pl.num_programs(1) - 1)
    def _():
        fin_ref[...] = st_ref[...]


def _s5(su, pp, l, s0, rows=512):
    nbg, _, n, _ = su.shape
    nt = n // rows
    fwd = pl.BlockSpec((None, N_SLAB, rows, LANES), lambda g, i: (g, 0, i, 0))
    bwd = pl.BlockSpec((None, N_SLAB, rows, LANES), lambda g, i: (g, 0, nt - 1 - i, 0))
    st = pl.BlockSpec((N_DIR, BG, SSM_FLAT), lambda g, i: (0, g, 0))
    return pl.pallas_call(
        _s5_kernel,
        grid=(nbg, nt),
        in_specs=[fwd, bwd, _const((N_DIR, SSM_SG, LANES, 2 * SG_STATE), (l,), single=True),
                  _const((N_DIR, SSM_SG, 2 * SG_STATE, LANES), (l,), single=True),
                  _const((N_DIR, BG, SSM_FLAT), (l,)), st],
        out_specs=[fwd, bwd, st],
        out_shape=[jax.ShapeDtypeStruct(su.shape, F32)] * 2 + [jax.ShapeDtypeStruct((N_DIR, nbg * BG, SSM_FLAT), F32)],
        scratch_shapes=[pltpu.VMEM((N_DIR, rows, SSM_FLAT), F32), pltpu.VMEM((N_DIR, rows, SSM_FLAT), F32),
                        pltpu.VMEM((N_DIR, BG, SSM_FLAT), F32)],
        compiler_params=_cparams(("arbitrary", "arbitrary")),
        name="s5_scan",
    )(su, su, pp['bm'], pp['cm'], pp['a_flat'], s0)


def _route(logits):
    rows = logits.shape[0]
    lane = lax.broadcasted_iota(jnp.int32, (rows, ROUTE_W), 1)
    neg_inf = -jnp.inf
    big = ROUTE_W

    def row_max(v):
        return jnp.max(v, axis=-1, keepdims=True)

    def first_lane(mask):
        return jnp.min(jnp.where(mask, lane, big), axis=-1, keepdims=True)

    in_g = lane < N_EXPERT_GROUPS
    gl = jnp.where(in_g, logits, neg_inf)
    g_max = row_max(gl)
    g_exp = jnp.where(in_g, jnp.exp(gl - g_max), 0.0)
    g_w = 1.0 / jnp.sum(g_exp, axis=-1, keepdims=True)
    g_sel = first_lane(gl == g_max)
    e_lo = N_EXPERT_GROUPS + g_sel * EXPERTS_PER_GROUP
    in_e = (lane >= e_lo) & (lane < e_lo + EXPERTS_PER_GROUP)
    el = jnp.where(in_e, logits, neg_inf)
    v1 = row_max(el)
    i1 = first_lane(el == v1)
    el2 = jnp.where(lane == i1, neg_inf, el)
    v2 = row_max(el2)
    i2 = first_lane(el2 == v2)
    e2 = jnp.exp(v2 - v1)
    w1 = g_w / (1.0 + e2)
    w2 = g_w * e2 / (1.0 + e2)
    comb = jnp.where(lane == i1, w1, jnp.where(lane == i2, w2, 0.0))
    return jnp.where(lane == GID_LANE, g_sel.astype(F32), comb)


def _merge_kernel(x_ref, o_ref, yr_ref, su_ref, yf_ref, yb_ref, mod_ref, g1_ref, g2_ref, *rest):
    gate_refs = rest[:N_GATE_BLK]
    (wa_ref, wr_ref, dsk_ref, wglu_ref, wout_ref, wrt_ref, brt_ref, x1_ref, h2_ref, comb_ref,
     h_scr, yr_scr, ys_scr, h2_scr) = rest[N_GATE_BLK:]
    tt = x_ref.shape[1]
    g1 = g1_ref[...]
    g2 = g2_ref[...]
    dsk = dsk_ref[...]
    per_gate = D_MODEL // GATE_BLK
    for b in range(BG):
        rb = slice(b * tt, (b + 1) * tt)
        h_scr[rb, :] = _modulate(x_ref[b], g1, mod_ref, b, 0, 1).astype(BF16)
        tm_rows = _strided_rows(b, tt)
        for c in range(N_SLAB):
            cs = slice(c * LANES, (c + 1) * LANES)
            yr_scr[rb, cs] = yr_ref.at[c][tm_rows, :].astype(BF16)
            ys = dsk[:, cs] * su_ref.at[c][tm_rows, :] + yf_ref.at[c][tm_rows, :] + yb_ref.at[c][tm_rows, :]
            ys_scr[rb, cs] = jax.nn.gelu(ys).astype(BF16)
    h = h_scr[...]
    y_attn = _dot(o_ref[...].reshape(BG * tt, ATTN_WIDTH).astype(BF16), wa_ref[...])
    y_rnn = _dot(yr_scr[...], wr_ref[...])
    glu = _dot(ys_scr[...], wglu_ref[...])
    y_ssm = glu[:, :D_MODEL] * _sigmoid(glu[:, D_MODEL:])
    mo = None
    for part in range(per_gate):
        cs = slice(part * GATE_BLK, (part + 1) * GATE_BLK)
        mix = None
        for j, y in enumerate((y_attn, y_rnn, y_ssm)):
            term = _sigmoid(_dot(h, gate_refs[j * per_gate + part][...])) * y[:, cs]
            mix = term if mix is None else mix + term
        contrib = _dot(mix.astype(BF16), wout_ref[cs, :])
        mo = contrib if mo is None else mo + contrib
    for b in range(BG):
        rb = slice(b * tt, (b + 1) * tt)
        x1 = x_ref[b] + mo[rb, :] * mod_ref[b, 2:3, :]
        x1_ref[b] = x1
        h2 = _modulate(x1, g2, mod_ref, b, 3, 4)
        h2_ref[b] = h2.astype(BF16)
        h2_scr[rb, :] = h2
    comb = _route(_dot3(h2_scr[...], wrt_ref[...]) + brt_ref[...])
    comb_ref[...] = comb.reshape(BG, tt, ROUTE_W)


def _merge(x, o, yr, su, yf, yb, mod, pp, l, tt=64):
    batch, seq, _ = x.shape
    nbg = batch // BG
    rows = BG * tt
    tok = lambda w: pl.BlockSpec((BG, tt, w), lambda g, i: (g, i, 0))
    slab = pl.BlockSpec((None, N_SLAB, rows, LANES), lambda g, i: (g, 0, i, 0))
    wspec = lambda *shape: _const(shape, (l,), single=True)
    gate_blk0 = PROJ_W // GATE_BLK
    gate = lambda j: pl.BlockSpec((None, D_MODEL, GATE_BLK), lambda *_: (l, 0, gate_blk0 + j),
                                  pipeline_mode=pl.Buffered(1))
    return pl.pallas_call(
        _merge_kernel,
        grid=(nbg, seq // tt),
        in_specs=[tok(D_MODEL), tok(ATTN_WIDTH), slab, slab, slab, slab,
                  _const((BG, N_MOD, D_MODEL)), _const((1, D_MODEL), (l,)), _const((1, D_MODEL), (l,))]
                 + [gate(j) for j in range(N_GATE_BLK)]
                 + [wspec(ATTN_WIDTH, D_MODEL), wspec(D_RNN, D_MODEL),
                  _const((1, D_SSM), (l,)), wspec(D_SSM, 2 * D_MODEL), wspec(D_MODEL, D_MODEL),
                  wspec(D_MODEL, ROUTE_W), _const((1, ROUTE_W), (l,))],
        out_specs=[tok(D_MODEL), tok(D_MODEL), tok(ROUTE_W)],
        out_shape=[jax.ShapeDtypeStruct((batch, seq, D_MODEL), F32), jax.ShapeDtypeStruct((batch, seq, D_MODEL), BF16),
                   jax.ShapeDtypeStruct((batch, seq, ROUTE_W), F32)],
        scratch_shapes=[pltpu.VMEM((rows, D_MODEL), BF16), pltpu.VMEM((rows, D_RNN), BF16),
                        pltpu.VMEM((rows, D_SSM), BF16), pltpu.VMEM((rows, D_MODEL), F32)],
        compiler_params=_cparams(("arbitrary", "arbitrary")),
        name="mixer_merge",
    )(x, o, yr, su, yf, yb, mod, pp['norm1'], pp['norm2'], *([pp['w_in']] * N_GATE_BLK), pp['w_attn_o'],
      pp['w_rnn_o'], pp['d_skip'], pp['w_glu'], pp['w_out'], pp['w_rt'], pp['b_rt'])


def _moe_kernel(x1_ref, h2_ref, comb_ref, mod_ref, wg_ref, wu_ref, wd_ref, *rest, final):
    if final:
        gf_ref, x2_ref, y_ref, hs_scr, cs_scr, acc_scr = rest
    else:
        x2_ref, hs_scr, cs_scr, acc_scr = rest
    t = MOE_T
    comb = comb_ref[...]
    lane = lax.broadcasted_iota(jnp.int32, (t, ROUTE_W), 1)
    gid = comb[:, GID_LANE:GID_LANE + 1]
    onehot = jnp.where((lane.astype(F32) == gid) & (lane < N_EXPERT_GROUPS), 1.0, 0.0)
    r_i = lax.broadcasted_iota(jnp.int32, (t, t), 0)
    c_i = lax.broadcasted_iota(jnp.int32, (t, t), 1)
    earlier = jnp.where(c_i < r_i, 1.0, 0.0).astype(BF16)
    csum = _dot(earlier, onehot.astype(BF16))
    rank = jnp.sum(onehot * csum, axis=-1, keepdims=True)
    counts = jnp.sum(onehot, axis=0, keepdims=True)
    lane_row = lax.broadcasted_iota(jnp.int32, (1, ROUTE_W), 1)
    cnt = [jnp.sum(jnp.where(lane_row == g, counts, 0.0)).astype(jnp.int32) for g in range(N_EXPERT_GROUPS)]
    start = [jnp.int32(0)]
    for g in range(1, N_EXPERT_GROUPS):
        start.append(start[-1] + cnt[g - 1])
    start_row = jnp.zeros((1, ROUTE_W), F32)
    for g in range(1, N_EXPERT_GROUPS):
        start_row = jnp.where(lane_row == g, start[g].astype(F32), start_row)
    pos = rank + jnp.sum(onehot * start_row, axis=-1, keepdims=True)
    col = lax.broadcasted_iota(jnp.int32, (t, MOE_TP), 1).astype(F32)
    pt = jnp.where(col == pos, 1.0, 0.0).astype(BF16)
    hs_scr[...] = _dot_tn(pt, h2_ref[...]).astype(BF16)
    c_hi, c_lo = _split_bf16(comb)
    cs_scr[...] = _dot_tn(pt, c_hi) + _dot_tn(pt, c_lo)
    acc_scr[...] = jnp.zeros_like(acc_scr)
    lane_w = lax.broadcasted_iota(jnp.int32, (MOE_W, ROUTE_W), 1)
    for g in range(N_EXPERT_GROUPS):
        base = (start[g] // ROW_ALIGN) * ROW_ALIGN
        n_win = (start[g] - base + cnt[g] + MOE_W - 1) // MOE_W

        def window(k, carry, g=g, base=base):
            r0 = pl.multiple_of(base + k * MOE_W, ROW_ALIGN)
            hs = hs_scr[pl.ds(r0, MOE_W), :]
            cw = cs_scr[pl.ds(r0, MOE_W), :]
            gate = _dot(hs, wg_ref[g])
            up = _dot(hs, wu_ref[g])
            parts = []
            for e in range(EXPERTS_PER_GROUP):
                sl = slice(e * D_EXPERT, (e + 1) * D_EXPERT)
                w_e = jnp.sum(jnp.where(lane_w == N_EXPERT_GROUPS + g * EXPERTS_PER_GROUP + e, cw, 0.0),
                              axis=-1, keepdims=True)
                ge = gate[:, sl]
                parts.append((ge * _sigmoid(ge) * up[:, sl] * w_e).astype(BF16))
            act = jnp.concatenate(parts, axis=1)
            acc_scr[pl.ds(r0, MOE_W), :] += _dot(act, wd_ref[g])
            return carry

        lax.fori_loop(0, n_win, window, 0)
    a_hi, a_lo = _split_bf16(acc_scr[...])
    moe = _dot(pt, a_hi) + _dot(pt, a_lo)
    x2 = x1_ref[...] + moe * mod_ref[5:6, :]
    x2_ref[...] = x2
    if final:
        y_ref[...] = _rms(x2, gf_ref[...])


def _moe(x1, h2, comb, mod, pp, l, seq, g_final):
    n = x1.shape[0]
    tm = MOE_T
    final = g_final is not None
    per_batch = mod.shape[0] > 1
    assert not per_batch or seq % tm == 0
    mod_idx = (lambda i: (i * tm // seq, 0, 0)) if per_batch else (lambda i: (0, 0, 0))
    rows = lambda w: pl.BlockSpec((tm, w), lambda i: (i, 0))
    wspec = _const((N_EXPERT_GROUPS, D_MODEL, GROUP_W), (l,), single=True)
    in_specs = [rows(D_MODEL), rows(D_MODEL), rows(ROUTE_W), pl.BlockSpec((None, N_MOD, D_MODEL), mod_idx),
                wspec, wspec, wspec]
    args = [x1, h2, comb, mod, pp['w_gate'], pp['w_up'], pp['w_down']]
    out_specs = [rows(D_MODEL)]
    out_shape = [jax.ShapeDtypeStruct((n, D_MODEL), F32)]
    if final:
        in_specs.append(pl.BlockSpec((1, D_MODEL), lambda i: (0, 0)))
        args.append(g_final)
        out_specs.append(rows(D_MODEL))
        out_shape.append(jax.ShapeDtypeStruct((n, D_MODEL), F32))
    return pl.pallas_call(
        functools.partial(_moe_kernel, final=final),
        grid=(n // tm,),
        in_specs=in_specs,
        out_specs=out_specs,
        out_shape=out_shape,
        scratch_shapes=[pltpu.VMEM((MOE_TP, D_MODEL), BF16), pltpu.VMEM((MOE_TP, ROUTE_W), F32),
                        pltpu.VMEM((MOE_TP, D_MODEL), F32)],
        compiler_params=_cparams(("arbitrary",)),
        name="hier_moe",
    )(*args)


def _head_mean_matrix(width):
    idx = np.arange(width) // HEAD_DIM
    return jnp.asarray((idx[:, None] == idx[None, :]).astype(np.float32) / HEAD_DIM, dtype=BF16)


def _rope_tables(seq):
    t = np.arange(seq)
    row = (t // GRID_W).astype(np.float64)
    col = (t % GRID_W).astype(np.float64)
    inv = ROPE_BASE ** (-np.arange(ROPE_PAIRS_PER_AXIS, dtype=np.float64) / ROPE_PAIRS_PER_AXIS)
    inv = inv.astype(np.float32).astype(np.float64)
    ang = np.concatenate([row[:, None] * inv, col[:, None] * inv], axis=-1)
    ang = ang.astype(np.float32).astype(np.float64)
    cos = np.repeat(np.cos(ang), 2, axis=-1)
    sin = np.repeat(np.sin(ang), 2, axis=-1) * np.tile(np.array([-1.0, 1.0]), HEAD_DIM // 2)
    expand = lambda a: jnp.asarray(np.tile(a, (1, LANES // HEAD_DIM)), dtype=F32)
    return expand(cos), expand(sin)


def _prep_params(p, s5p):
    bf = lambda a: a.astype(BF16)
    pp = {}
    pp['w_in'] = bf(p['w_in'])
    pp['norm1'] = p['norm1'][:, None, :]
    pp['norm2'] = p['norm2'][:, None, :]
    pp['qg'] = jnp.tile(p['q_norm'], (1, N_HEADS))[:, None, :]
    pp['kg'] = jnp.tile(p['k_norm'], (1, N_KV_HEADS))[:, None, :]
    pp['s_q'] = _head_mean_matrix(ATTN_WIDTH)
    pp['s_k'] = _head_mean_matrix(KV_WIDTH)
    pp['w_attn_o'] = bf(p['w_attn_o'])
    pp['w_rnn_o'] = bf(p['w_rnn_o'])
    pp['w_glu'] = bf(p['w_glu'])
    pp['w_out'] = bf(p['w_out'])
    pp['d_skip'] = p['ssm_d'][:, None, :]
    pp['conv_w'] = p['conv_w']
    pp['conv_b'] = p['conv_b'][:, None, :]
    pp['lam'] = p['lru_lambda']
    per = LANES // RNN_BLOCK_W
    wa, wi = p['lru_w_a'], p['lru_w_i']
    w = jnp.stack([wa[:, 0], wi[:, 0], wa[:, 1], wi[:, 1]], axis=1)
    w = w.reshape(DEPTH, 4, N_SLAB, per, RNN_BLOCK_W, RNN_BLOCK_W)
    eye = jnp.eye(per, dtype=F32)
    wg = w.transpose(0, 2, 3, 4, 1, 5)[:, :, :, :, :, None, :] * eye[None, None, :, None, None, :, None]
    pp['wg'] = bf(wg.reshape(DEPTH, N_SLAB, LANES, 4 * LANES))
    ba, bi = p['lru_b_a'], p['lru_b_i']
    b = jnp.stack([ba[:, 0], bi[:, 0], ba[:, 1], bi[:, 1]], axis=1).reshape(DEPTH, 4, N_SLAB, LANES)
    pp['bg'] = b.transpose(0, 2, 1, 3).reshape(DEPTH, N_SLAB, 1, 4 * LANES)
    ab_re, ab_im, bb_re, bb_im = s5p
    eye_g = jnp.eye(SG_GROUPS, dtype=F32)
    lead = (DEPTH, N_DIR, SSM_SG, SG_GROUPS)
    bb = jnp.stack([bb_re, bb_im], axis=2).reshape(lead + (SSM_GROUP_W, 2, SSM_STATE))
    eye_g = bf(eye_g)
    bm = bf(bb)[:, :, :, :, :, :, None, :] * eye_g[None, None, None, :, None, None, :, None]
    pp['bm'] = bm.reshape(DEPTH, N_DIR, SSM_SG, LANES, 2 * SG_STATE)
    cc = jnp.stack([p['ssm_c_re'], -p['ssm_c_im']], axis=2)
    cc = cc.reshape(DEPTH, N_DIR, 2, SSM_SG, SG_GROUPS, SSM_GROUP_W, SSM_STATE).transpose(0, 1, 3, 2, 4, 6, 5)
    cm = bf(cc)[:, :, :, :, :, :, None, :] * eye_g[None, None, None, None, :, None, :, None]
    pp['cm'] = cm.reshape(DEPTH, N_DIR, SSM_SG, 2 * SG_STATE, LANES)
    are = ab_re.reshape(DEPTH, N_DIR, SSM_SG, SG_STATE)
    aim = ab_im.reshape(DEPTH, N_DIR, SSM_SG, SG_STATE)
    a_flat = jnp.concatenate([are, aim], axis=-1).reshape(DEPTH, N_DIR, 1, SSM_FLAT)
    pp['a_flat'] = jnp.broadcast_to(a_flat, (DEPTH, N_DIR, BG, SSM_FLAT))
    fill = ROUTE_W - N_EXPERT_GROUPS - N_EXPERTS
    pp['w_rt'] = jnp.concatenate([p['router_g_w'], p['router_e_w'], jnp.zeros((DEPTH, D_MODEL, fill), F32)], axis=-1)
    pp['b_rt'] = jnp.concatenate([p['router_g_b'], p['router_e_b'], jnp.zeros((DEPTH, fill), F32)], axis=-1)[:, None, :]
    side = lambda w: bf(w).reshape(DEPTH, N_EXPERT_GROUPS, EXPERTS_PER_GROUP, D_MODEL, D_EXPERT).transpose(
        0, 1, 3, 2, 4).reshape(DEPTH, N_EXPERT_GROUPS, D_MODEL, GROUP_W)
    pp['w_gate'] = side(p['w_e_gate'])
    pp['w_up'] = side(p['w_e_up'])
    pp['w_down'] = bf(p['w_e_down']).reshape(DEPTH, N_EXPERT_GROUPS, GROUP_W, D_MODEL)
    return pp


def _flat_state(re, im):
    b = re.shape[0]
    r = re.transpose(1, 2, 0, 3, 4).reshape(DEPTH, N_DIR, b, SSM_SG, SG_STATE)
    i = im.transpose(1, 2, 0, 3, 4).reshape(DEPTH, N_DIR, b, SSM_SG, SG_STATE)
    return jnp.concatenate([r, i], axis=-1).reshape(DEPTH, N_DIR, b, SSM_FLAT)


def _unflat_state(flat):
    b = flat.shape[2]
    f = flat.reshape(DEPTH, N_DIR, b, SSM_SG, 2, SG_GROUPS, SSM_STATE).transpose(4, 2, 0, 1, 3, 5, 6)
    f = f.reshape(2, b, DEPTH, N_DIR, SSM_GROUPS, SSM_STATE)
    return f[0], f[1]


def _trunk_layer(x, mod, mod_moe, pp, l, ctx, rope_tabs, g_final, attn_tq):
    batch, seq, _ = x.shape
    q, k, v, rx, rg, su = _in_proj(x, mod, pp, l, rope_tabs)
    if ctx is None:
        attn_ctx = None
        h0 = jnp.zeros((N_DIR, batch, D_RNN), F32)
        s0 = jnp.zeros((N_DIR, batch, SSM_FLAT), F32)
    else:
        cache_k, cache_v, h0, s0 = ctx
        attn_ctx = (cache_k, cache_v, l)
    o = _attention(q, k, v, attn_ctx, attn_tq)
    yr, rnn_fin = _rglru(rx, rg, pp, l, h0)
    yf, yb, ssm_fin = _s5(su, pp, l, s0)
    x1, h2, comb = _merge(x, o, yr, su, yf, yb, mod, pp, l)
    flat = lambda a: a.reshape(batch * seq, a.shape[-1])
    outs = _moe(flat(x1), flat(h2), flat(comb), mod_moe, pp, l, seq, g_final)
    return [a.reshape(batch, seq, D_MODEL) for a in outs], (k, v, rnn_fin, ssm_fin)


def kernel(x_prompt, x_sample, cache_k, cache_v, state_rglru, state_ssm_re, state_ssm_im, c, c_ctx, w_mod, b_mod, norm1, norm2, w_in, q_norm, k_norm, w_attn_o, conv_w, conv_b, lru_w_a, lru_b_a, lru_w_i, lru_b_i, lru_lambda, w_rnn_o, ssm_lam_re, ssm_lam_im, ssm_log_step, ssm_b_re, ssm_b_im, ssm_c_re, ssm_c_im, ssm_d, w_glu, w_out, router_g_w, router_g_b, router_e_w, router_e_b, w_e_gate, w_e_up, w_e_down, final_norm):
    p = dict(norm1=norm1, norm2=norm2, w_in=w_in, q_norm=q_norm, k_norm=k_norm, w_attn_o=w_attn_o, conv_w=conv_w,
             conv_b=conv_b, lru_w_a=lru_w_a, lru_b_a=lru_b_a, lru_w_i=lru_w_i, lru_b_i=lru_b_i,
             lru_lambda=lru_lambda, w_rnn_o=w_rnn_o, ssm_c_re=ssm_c_re, ssm_c_im=ssm_c_im, ssm_d=ssm_d, w_glu=w_glu,
             w_out=w_out, router_g_w=router_g_w, router_g_b=router_g_b, router_e_w=router_e_w,
             router_e_b=router_e_b, w_e_gate=w_e_gate, w_e_up=w_e_up, w_e_down=w_e_down)
    bc, lc, _ = x_prompt.shape
    bd, ld, _ = x_sample.shape
    past = cache_k.shape[2]
    assert bd == BG and bc % BG == 0

    n_cond = 1 + bd
    cond_rows = -(-n_cond // SUBLANES) * SUBLANES
    cond = jnp.concatenate([c_ctx[None], c, jnp.zeros((cond_rows - n_cond, D_MODEL), F32)], axis=0)
    mods = _modulation(cond, w_mod, b_mod).reshape(DEPTH, cond_rows, N_MOD, D_MODEL)
    s5p = _s5_params(ssm_lam_re, ssm_lam_im, ssm_log_step, ssm_b_re, ssm_b_im)
    pp = _prep_params(p, s5p)
    rope_tabs = _rope_tables(ld)
    final_g = final_norm[None]

    ck = cache_k.reshape(bd, DEPTH, past, KV_WIDTH)
    cv = cache_v.reshape(bd, DEPTH, past, KV_WIDTH)
    h0_lat = state_rglru.transpose(1, 2, 0, 3)
    s0_lat = _flat_state(state_ssm_re, state_ssm_im)
    x_ctx, x_lat = x_prompt, x_sample
    ks, vs, rs, ss = [], [], [], []
    y_ctx = y_lat = None
    for l in range(DEPTH):
        g_final = final_g if l == DEPTH - 1 else None
        mod_ctx1 = mods[l, 0:1]
        mod_ctx = jnp.broadcast_to(mod_ctx1, (BG, N_MOD, D_MODEL))
        mod_lat = mods[l, 1:n_cond]
        outs, (k_l, v_l, r_l, s_l) = _trunk_layer(x_ctx, mod_ctx, mod_ctx1, pp, l, None, None, g_final, lc)
        x_ctx = outs[0]
        if g_final is not None:
            y_ctx = outs[1]
        ks.append(k_l)
        vs.append(v_l)
        rs.append(r_l)
        ss.append(s_l)
        ctx = (ck, cv, h0_lat[l], s0_lat[l])
        outs, _ = _trunk_layer(x_lat, mod_lat, mod_lat, pp, l, ctx, rope_tabs, g_final, 512)
        x_lat = outs[0]
        if g_final is not None:
            y_lat = outs[1]
    new_k = jnp.stack(ks, axis=1).reshape(bc, DEPTH, lc, N_KV_HEADS, HEAD_DIM)
    new_v = jnp.stack(vs, axis=1).reshape(bc, DEPTH, lc, N_KV_HEADS, HEAD_DIM)
    new_r = jnp.stack(rs, axis=0).transpose(2, 0, 1, 3)
    new_re, new_im = _unflat_state(jnp.stack(ss, axis=0))
    return (y_ctx, y_lat, new_k, new_v, new_r, new_re, new_im)
```

```python
import functools

import numpy as np
import jax
import jax.numpy as jnp
from jax import lax
from jax.experimental import pallas as pl
from jax.experimental.pallas import tpu as pltpu

F32 = jnp.float32
BF16 = jnp.bfloat16

D_MODEL = 1024
DEPTH = 2
GRID_W = 64
N_DIR = 2
NORM_EPS = 1e-6
N_MOD = 6

N_HEADS = 8
N_KV_HEADS = 4
HEAD_DIM = 64
ATTN_WIDTH = N_HEADS * HEAD_DIM
KV_WIDTH = N_KV_HEADS * HEAD_DIM
ROPE_BASE = 10000.0
ROPE_PAIRS_PER_AXIS = HEAD_DIM // 4

D_RNN = D_MODEL // 2
RNN_BLOCKS = 8
RNN_BLOCK_W = D_RNN // RNN_BLOCKS
CONV_W = 4
CONV_LEFT = 2
LRU_C = 8.0

D_SSM = D_MODEL // 2
SSM_GROUP_W = 16
SSM_GROUPS = D_SSM // SSM_GROUP_W
SSM_STATE = 64

N_EXPERT_GROUPS = 4
EXPERTS_PER_GROUP = 4
N_EXPERTS = N_EXPERT_GROUPS * EXPERTS_PER_GROUP
D_EXPERT = D_MODEL // 4

LANES = 128
SUBLANES = 8
VMEM_LIMIT = 56 * 1024 * 1024

BG = SUBLANES
PROJ_W = ATTN_WIDTH + 2 * KV_WIDTH + 2 * D_RNN + D_SSM
N_SLAB = D_RNN // LANES
GATE_BLK = 512
N_GATE_BLK = 3 * D_MODEL // GATE_BLK
assert PROJ_W % GATE_BLK == 0

SG_GROUPS = LANES // SSM_GROUP_W
SSM_SG = SSM_GROUPS // SG_GROUPS
SG_STATE = SG_GROUPS * SSM_STATE
SSM_FLAT = SSM_SG * 2 * SG_STATE

ROUTE_W = LANES
GID_LANE = ROUTE_W - 1
MOE_T = 512
MOE_W = 176
ROW_ALIGN = 16
MOE_TP = 768
assert MOE_TP >= MOE_T + MOE_W + ROW_ALIGN and MOE_TP % LANES == 0 and MOE_W % ROW_ALIGN == 0


def _cparams(sem, vmem=VMEM_LIMIT):
    return pltpu.CompilerParams(dimension_semantics=sem, vmem_limit_bytes=vmem)


def _const(shape, lead=(), single=False):
    idx = tuple(lead) + (0,) * len(shape)
    block = (None,) * len(lead) + tuple(shape)
    if single:
        return pl.BlockSpec(block, lambda *_: idx, pipeline_mode=pl.Buffered(1))
    return pl.BlockSpec(block, lambda *_: idx)


def _dot(a, b):
    return jnp.dot(a, b, preferred_element_type=F32)


def _dot_nt(a, b):
    return lax.dot_general(a, b, (((1,), (1,)), ((), ())), preferred_element_type=F32)


def _dot_tn(a, b):
    return lax.dot_general(a, b, (((0,), (0,)), ((), ())), preferred_element_type=F32)


def _split_bf16(a):
    hi = a.astype(BF16)
    lo = (a - hi.astype(F32)).astype(BF16)
    return hi, lo


def _dot3(a, b):
    a_hi, a_lo = _split_bf16(a)
    b_hi, b_lo = _split_bf16(b)
    return _dot(a_hi, b_hi) + (_dot(a_lo, b_hi) + _dot(a_hi, b_lo))


def _sigmoid(x):
    return 0.5 * jnp.tanh(0.5 * x) + 0.5


def _rms(x, g):
    ms = jnp.mean(x * x, axis=-1, keepdims=True)
    return x * lax.rsqrt(ms + NORM_EPS) * g


def _modulate(x, g, mod_ref, b, shift_i, scale_i):
    return _rms(x, g) * (1.0 + mod_ref[b, scale_i:scale_i + 1, :]) + mod_ref[b, shift_i:shift_i + 1, :]


def _strided_rows(b, n):
    return pl.ds(b, n, stride=BG)


def _mod_kernel(c_ref, w_ref, b_ref, o_ref):
    c = c_ref[...]
    a = c * _sigmoid(c)
    o_ref[...] = _dot3(a, w_ref[...]) + b_ref[...]


def _modulation(cond, w_mod, b_mod):
    rows = cond.shape[0]
    tn = 1536
    width = N_MOD * D_MODEL
    return pl.pallas_call(
        _mod_kernel,
        grid=(DEPTH, width // tn),
        in_specs=[
            pl.BlockSpec((rows, D_MODEL), lambda l, j: (0, 0)),
            pl.BlockSpec((None, D_MODEL, tn), lambda l, j: (l, 0, j)),
            pl.BlockSpec((None, 1, tn), lambda l, j: (l, 0, j)),
        ],
        out_specs=pl.BlockSpec((None, rows, tn), lambda l, j: (l, 0, j)),
        out_shape=jax.ShapeDtypeStruct((DEPTH, rows, width), F32),
        compiler_params=_cparams(("arbitrary", "arbitrary")),
        name="adaln_modulation",
    )(cond, w_mod, b_mod.reshape(DEPTH, 1, width))


def _head_norm(z, s_ref, gain):
    ms = _dot((z * z).astype(BF16), s_ref[...])
    return z * lax.rsqrt(ms + NORM_EPS) * gain


def _rope(z, cos, sin):
    rows = z.shape[0]
    tt = cos.shape[0]
    lane = lax.broadcasted_iota(jnp.int32, (rows, LANES), 1)
    partner = jnp.where((lane & 1) == 0, pltpu.roll(z, LANES - 1, 1), pltpu.roll(z, 1, 1))
    z3 = z.reshape(rows // tt, tt, LANES)
    p3 = partner.reshape(rows // tt, tt, LANES)
    return z3 * cos[None] + p3 * sin[None]


def _in_proj_kernel(x_ref, mod_ref, g_ref, w_ref, qg_ref, kg_ref, sq_ref, sk_ref, *rest, rope):
    if rope:
        cos_ref, sin_ref, q_ref, k_ref, v_ref, rx_ref, rg_ref, su_ref, h_scr = rest
    else:
        q_ref, k_ref, v_ref, rx_ref, rg_ref, su_ref, h_scr = rest
    tt = x_ref.shape[1]
    g = g_ref[...]
    qscale = HEAD_DIM ** -0.5
    for b in range(BG):
        h_scr[b * tt:(b + 1) * tt, :] = _modulate(x_ref[b], g, mod_ref, b, 0, 1).astype(BF16)
    h = h_scr[...]

    def proj(lo, width):
        return _dot(h, w_ref[:, lo:lo + width])

    q = _head_norm(proj(0, ATTN_WIDTH), sq_ref, qg_ref[...]) * qscale
    k = _head_norm(proj(ATTN_WIDTH, KV_WIDTH), sk_ref, kg_ref[...])
    if rope:
        cos = cos_ref[...]
        sin = sin_ref[...]
        for dst, z in ((q_ref, q), (k_ref, k)):
            for c in range(z.shape[1] // LANES):
                cs = slice(c * LANES, (c + 1) * LANES)
                dst[:, :, cs] = _rope(z[:, cs], cos, sin)
    else:
        q_ref[...] = q.reshape(BG, tt, ATTN_WIDTH)
        k_ref[...] = k.reshape(BG, tt, KV_WIDTH)
    off = ATTN_WIDTH + KV_WIDTH
    v_ref[...] = proj(off, KV_WIDTH).reshape(BG, tt, KV_WIDTH)
    off += KV_WIDTH
    for dst in (rx_ref, rg_ref, su_ref):
        z = proj(off, D_RNN)
        off += D_RNN
        for b in range(BG):
            for c in range(N_SLAB):
                dst.at[c][_strided_rows(b, tt), :] = z[b * tt:(b + 1) * tt, c * LANES:(c + 1) * LANES]


def _in_proj(x, mod, pp, l, rope_tabs, tt=64):
    batch, seq, _ = x.shape
    nbg = batch // BG
    rope = rope_tabs is not None
    tok = lambda w: pl.BlockSpec((BG, tt, w), lambda g, i: (g, i, 0))
    slab = pl.BlockSpec((None, N_SLAB, tt * BG, LANES), lambda g, i: (g, 0, i, 0))
    in_specs = [tok(D_MODEL), _const((BG, N_MOD, D_MODEL)), _const((1, D_MODEL), (l,)),
                pl.BlockSpec((None, D_MODEL, PROJ_W), lambda *_: (l, 0, 0), pipeline_mode=pl.Buffered(1)),
                _const((1, ATTN_WIDTH), (l,)),
                _const((1, KV_WIDTH), (l,)), _const((ATTN_WIDTH, ATTN_WIDTH)), _const((KV_WIDTH, KV_WIDTH))]
    args = [x, mod, pp['norm1'], pp['w_in'], pp['qg'], pp['kg'], pp['s_q'], pp['s_k']]
    if rope:
        tab = pl.BlockSpec((tt, LANES), lambda g, i: (i, 0))
        in_specs += [tab, tab]
        args += list(rope_tabs)
    tok_shape = lambda w: jax.ShapeDtypeStruct((batch, seq, w), F32)
    slab_shape = jax.ShapeDtypeStruct((nbg, N_SLAB, seq * BG, LANES), F32)
    return pl.pallas_call(
        functools.partial(_in_proj_kernel, rope=rope),
        grid=(nbg, seq // tt),
        in_specs=in_specs,
        out_specs=[tok(ATTN_WIDTH), tok(KV_WIDTH), tok(KV_WIDTH), slab, slab, slab],
        out_shape=[tok_shape(ATTN_WIDTH), tok_shape(KV_WIDTH), tok_shape(KV_WIDTH), slab_shape, slab_shape, slab_shape],
        scratch_shapes=[pltpu.VMEM((BG * tt, D_MODEL), BF16)],
        compiler_params=_cparams(("arbitrary", "arbitrary")),
        name="mixer_in_proj",
    )(*args)


def _attn_kernel(q_ref, k_ref, v_ref, *rest, n_ctx):
    if n_ctx:
        kc_ref, vc_ref, o_ref, kd_ref, va_ref, vb_ref = rest
    else:
        o_ref, kd_ref, va_ref, vb_ref = rest
    lk = k_ref.shape[0]
    tq = q_ref.shape[0]

    def lane_lt(rows):
        return lax.broadcasted_iota(jnp.int32, (rows, LANES), 1) < HEAD_DIM

    @pl.when(pl.program_id(1) == 0)
    def _():
        def fill(src_k, src_v, base, rows):
            low = lane_lt(rows)
            for pair in range(N_KV_HEADS // 2):
                sl = slice(pair * LANES, (pair + 1) * LANES)
                k2 = src_k[:, sl]
                v2 = src_v[:, sl]
                k2r = pltpu.roll(k2, HEAD_DIM, 1)
                v2r = pltpu.roll(v2, HEAD_DIM, 1)
                for odd in range(2):
                    j = 2 * pair + odd
                    kd = jnp.where(low, k2r, k2) if odd else jnp.where(low, k2, k2r)
                    vd = jnp.where(low, v2r, v2) if odd else jnp.where(low, v2, v2r)
                    kd_ref[j, base:base + rows, :] = kd.astype(BF16)
                    va_ref[j, base:base + rows, :] = jnp.where(low, vd, 0.0).astype(BF16)
                    vb_ref[j, base:base + rows, :] = jnp.where(low, 0.0, vd).astype(BF16)

        if n_ctx:
            fill(kc_ref, vc_ref, 0, n_ctx)
        fill(k_ref, v_ref, n_ctx, lk)

    low = lane_lt(tq)
    for j in range(N_KV_HEADS):
        q2 = q_ref[:, j * LANES:(j + 1) * LANES]
        kd = kd_ref[j]
        s0 = _dot_nt(jnp.where(low, q2, 0.0).astype(BF16), kd)
        s1 = _dot_nt(jnp.where(low, 0.0, q2).astype(BF16), kd)
        p0 = jnp.exp(s0 - jnp.max(s0, axis=-1, keepdims=True))
        p1 = jnp.exp(s1 - jnp.max(s1, axis=-1, keepdims=True))
        r0 = 1.0 / jnp.sum(p0, axis=-1, keepdims=True)
        r1 = 1.0 / jnp.sum(p1, axis=-1, keepdims=True)
        o2 = _dot(p0.astype(BF16), va_ref[j]) + _dot(p1.astype(BF16), vb_ref[j])
        o_ref[:, j * LANES:(j + 1) * LANES] = o2 * jnp.where(low, r0, r1)


def _attention(q, k, v, ctx, tq):
    batch, seq, _ = q.shape
    n_ctx = 0 if ctx is None else ctx[0].shape[2]
    tok = lambda w: pl.BlockSpec((None, seq, w), lambda b, i: (b, 0, 0))
    in_specs = [pl.BlockSpec((None, tq, ATTN_WIDTH), lambda b, i: (b, i, 0)), tok(KV_WIDTH), tok(KV_WIDTH)]
    args = [q, k, v]
    if ctx is not None:
        cache_k, cache_v, layer = ctx
        cspec = pl.BlockSpec((None, None, n_ctx, KV_WIDTH), lambda b, i: (b, layer, 0, 0))
        in_specs += [cspec, cspec]
        args += [cache_k, cache_v]
    lt = n_ctx + seq
    return pl.pallas_call(
        functools.partial(_attn_kernel, n_ctx=n_ctx),
        grid=(batch, seq // tq),
        in_specs=in_specs,
        out_specs=pl.BlockSpec((None, tq, ATTN_WIDTH), lambda b, i: (b, i, 0)),
        out_shape=jax.ShapeDtypeStruct((batch, seq, ATTN_WIDTH), F32),
        scratch_shapes=[pltpu.VMEM((N_KV_HEADS, lt, LANES), BF16)] * 3,
        compiler_params=_cparams(("arbitrary", "arbitrary")),
        name="gqa_attention",
    )(*args)


def _rglru_kernel(rx_ref, rg_ref, cw_ref, cb_ref, wg_ref, bg_ref, lam_ref, h0_ref, y_ref, fin_ref,
                  xpad_ref, af_ref, uf_ref, ab_ref, ub_ref, *, seq, chunk):
    n = seq * BG
    pad = CONV_LEFT * BG
    n_chunks = n // chunk
    zeros = jnp.zeros((pad, LANES), F32)
    xpad_ref[0:pad, :] = zeros
    xpad_ref[pad + n:pad + n + pad, :] = zeros

    def copy_in(i, c):
        r0 = pl.multiple_of(i * chunk, chunk)
        xpad_ref[pl.ds(pad + r0, chunk), :] = rx_ref[pl.ds(r0, chunk), :]
        return c

    lax.fori_loop(0, n_chunks, copy_in, 0)

    neg = -lam_ref[...]
    softplus = jnp.maximum(neg, 0.0) + jnp.log1p(jnp.exp(-jnp.abs(neg)))
    decay = -LRU_C * softplus
    cw = cw_ref[...]
    cb = cb_ref[...]

    def gates(i, c):
        r0 = pl.multiple_of(i * chunk, chunk)
        xc = cb
        for j in range(CONV_W):
            xc = xc + xpad_ref[pl.ds(r0 + j * BG, chunk), :] * cw[j:j + 1, :]
        g = _dot(xc.astype(BF16), wg_ref[...]) + bg_ref[...]
        for d, (a_ref, u_ref) in enumerate(((af_ref, uf_ref), (ab_ref, ub_ref))):
            r = _sigmoid(g[:, (2 * d) * LANES:(2 * d + 1) * LANES])
            gi = _sigmoid(g[:, (2 * d + 1) * LANES:(2 * d + 2) * LANES])
            log_a = r * decay[d:d + 1, :]
            a_ref[pl.ds(r0, chunk), :] = jnp.exp(log_a)
            th = jnp.tanh(log_a)
            w = -2.0 * th
            mult = jnp.where(w > 0.0, w * lax.rsqrt(w * (1.0 - th)), 0.0)
            u_ref[pl.ds(r0, chunk), :] = mult * (gi * xc)
        return c

    lax.fori_loop(0, n_chunks, gates, 0)

    def step(t, carry):
        hf, hb = carry
        rf = pl.multiple_of(t * BG, BG)
        rb = pl.multiple_of((seq - 1 - t) * BG, BG)
        hf = af_ref[pl.ds(rf, BG), :] * hf + uf_ref[pl.ds(rf, BG), :]
        y_ref[pl.ds(rf, BG), :] = hf
        hb = ab_ref[pl.ds(rb, BG), :] * hb + ub_ref[pl.ds(rb, BG), :]
        xpad_ref[pl.ds(rb, BG), :] = hb
        return hf, hb

    hf, hb = lax.fori_loop(0, seq, step, (h0_ref[0], h0_ref[1]), unroll=8)
    fin_ref[0] = hf
    fin_ref[1] = hb

    def emit(i, c):
        r0 = pl.multiple_of(i * chunk, chunk)
        gate = jax.nn.gelu(rg_ref[pl.ds(r0, chunk), :])
        y_ref[pl.ds(r0, chunk), :] = gate * (y_ref[pl.ds(r0, chunk), :] + xpad_ref[pl.ds(r0, chunk), :])
        return c

    lax.fori_loop(0, n_chunks, emit, 0)


def _rglru(rx, rg, pp, l, h0, chunk=512):
    nbg, _, n, _ = rx.shape
    seq = n // BG
    slab = pl.BlockSpec((None, None, n, LANES), lambda g, c: (g, c, 0, 0))
    col = lambda rows: pl.BlockSpec((None, rows, LANES), lambda g, c: (l, 0, c))
    state = pl.BlockSpec((N_DIR, BG, LANES), lambda g, c: (0, g, c))
    return pl.pallas_call(
        functools.partial(_rglru_kernel, seq=seq, chunk=chunk),
        grid=(nbg, N_SLAB),
        in_specs=[slab, slab, col(CONV_W), col(1),
                  pl.BlockSpec((None, None, LANES, 4 * LANES), lambda g, c: (l, c, 0, 0)),
                  pl.BlockSpec((None, None, 1, 4 * LANES), lambda g, c: (l, c, 0, 0)),
                  col(N_DIR), state],
        out_specs=[slab, state],
        out_shape=[jax.ShapeDtypeStruct(rx.shape, F32), jax.ShapeDtypeStruct((N_DIR, nbg * BG, D_RNN), F32)],
        scratch_shapes=[pltpu.VMEM((n + 2 * CONV_LEFT * BG, LANES), F32)] + [pltpu.VMEM((n, LANES), F32)] * 4,
        compiler_params=_cparams(("arbitrary", "arbitrary")),
        name="rglru_scan",
    )(rx, rg, pp['conv_w'], pp['conv_b'], pp['wg'], pp['bg'], pp['lam'], h0)


def _s5_param_kernel(lr_ref, li_ref, ls_ref, br_ref, bi_ref, are_ref, aim_ref, bbr_ref, bbi_ref):
    lr = lr_ref[...]
    li = li_ref[...]
    step = jnp.exp(ls_ref[...])
    mag = jnp.exp(lr * step)
    ab_re = mag * jnp.cos(li * step)
    ab_im = mag * jnp.sin(li * step)
    den = lr * lr + li * li
    nr = ab_re - 1.0
    ni = ab_im
    f_re = (nr * lr + ni * li) / den
    f_im = (ni * lr - nr * li) / den
    are_ref[...] = ab_re
    aim_ref[...] = ab_im
    br = br_ref[...]
    bi = bi_ref[...]
    bbr_ref[...] = f_re[:, None, :] * br - f_im[:, None, :] * bi
    bbi_ref[...] = f_re[:, None, :] * bi + f_im[:, None, :] * br


def _s5_params(lam_re, lam_im, log_step, b_re, b_im):
    rows = DEPTH * N_DIR * SSM_GROUPS
    flat = lambda a: a.reshape(rows, SSM_STATE)
    bt = lambda a: a.reshape(rows, SSM_STATE, SSM_GROUP_W).transpose(0, 2, 1)
    vec = jax.ShapeDtypeStruct((rows, SSM_STATE), F32)
    mat = jax.ShapeDtypeStruct((rows, SSM_GROUP_W, SSM_STATE), F32)
    return pl.pallas_call(
        _s5_param_kernel,
        out_shape=[vec, vec, mat, mat],
        name="s5_discretise",
    )(flat(lam_re), flat(lam_im), log_step.reshape(rows, 1), bt(b_re), bt(b_im))


def _s5_kernel(uf_ref, ub_ref, bm_ref, cm_ref, a_ref, s0_ref, yf_ref, yb_ref, fin_ref, bu_ref, xs_ref, st_ref):
    rows = uf_ref.shape[1]
    steps = rows // BG
    i = pl.program_id(1)

    @pl.when(i == 0)
    def _():
        st_ref[...] = s0_ref[...]

    for d, u_ref in enumerate((uf_ref, ub_ref)):
        for s in range(SSM_SG):
            bu_ref[d, :, s * 2 * SG_STATE:(s + 1) * 2 * SG_STATE] = _dot(u_ref[s].astype(BF16), bm_ref[d, s])

    pair_cols = [(s * 2 * SG_STATE + j * LANES, s * 2 * SG_STATE + SG_STATE + j * LANES)
                 for s in range(SSM_SG) for j in range(SG_STATE // LANES)]
    for d in range(N_DIR):
        state = [(st_ref[d, :, cr:cr + LANES], st_ref[d, :, ci:ci + LANES]) for cr, ci in pair_cols]
        for t in (range(steps) if d == 0 else range(steps - 1, -1, -1)):
            r0 = t * BG
            for p, (cr, ci) in enumerate(pair_cols):
                xr, xi = state[p]
                ar = a_ref[d, :, cr:cr + LANES]
                ai = a_ref[d, :, ci:ci + LANES]
                nr = ar * xr - ai * xi + bu_ref[d, r0:r0 + BG, cr:cr + LANES]
                ni = ar * xi + ai * xr + bu_ref[d, r0:r0 + BG, ci:ci + LANES]
                xs_ref[d, r0:r0 + BG, cr:cr + LANES] = nr
                xs_ref[d, r0:r0 + BG, ci:ci + LANES] = ni
                state[p] = (nr, ni)
        for (cr, ci), (xr, xi) in zip(pair_cols, state):
            st_ref[d, :, cr:cr + LANES] = xr
            st_ref[d, :, ci:ci + LANES] = xi

    for d, y_ref in enumerate((yf_ref, yb_ref)):
        for s in range(SSM_SG):
            xs = xs_ref[d, :, s * 2 * SG_STATE:(s + 1) * 2 * SG_STATE].astype(BF16)
            y_ref[s] = _dot(xs, cm_ref[d, s])

    @pl.when(i == pl.num_programs(1) - 1)
    def _():
        fin_ref[...] = st_ref[...]


def _s5(su, pp, l, s0, rows=512):
    nbg, _, n, _ = su.shape
    nt = n // rows
    fwd = pl.BlockSpec((None, N_SLAB, rows, LANES), lambda g, i: (g, 0, i, 0))
    bwd = pl.BlockSpec((None, N_SLAB, rows, LANES), lambda g, i: (g, 0, nt - 1 - i, 0))
    st = pl.BlockSpec((N_DIR, BG, SSM_FLAT), lambda g, i: (0, g, 0))
    return pl.pallas_call(
        _s5_kernel,
        grid=(nbg, nt),
        in_specs=[fwd, bwd, _const((N_DIR, SSM_SG, LANES, 2 * SG_STATE), (l,), single=True),
                  _const((N_DIR, SSM_SG, 2 * SG_STATE, LANES), (l,), single=True),
                  _const((N_DIR, BG, SSM_FLAT), (l,)), st],
        out_specs=[fwd, bwd, st],
        out_shape=[jax.ShapeDtypeStruct(su.shape, F32)] * 2 + [jax.ShapeDtypeStruct((N_DIR, nbg * BG, SSM_FLAT), F32)],
        scratch_shapes=[pltpu.VMEM((N_DIR, rows, SSM_FLAT), F32), pltpu.VMEM((N_DIR, rows, SSM_FLAT), F32),
                        pltpu.VMEM((N_DIR, BG, SSM_FLAT), F32)],
        compiler_params=_cparams(("arbitrary", "arbitrary")),
        name="s5_scan",
    )(su, su, pp['bm'], pp['cm'], pp['a_flat'], s0)


def _route(logits):
    rows = logits.shape[0]
    lane = lax.broadcasted_iota(jnp.int32, (rows, ROUTE_W), 1)
    neg_inf = -jnp.inf
    big = ROUTE_W

    def row_max(v):
        return jnp.max(v, axis=-1, keepdims=True)

    def first_lane(mask):
        return jnp.min(jnp.where(mask, lane, big), axis=-1, keepdims=True)

    in_g = lane < N_EXPERT_GROUPS
    gl = jnp.where(in_g, logits, neg_inf)
    g_max = row_max(gl)
    g_exp = jnp.where(in_g, jnp.exp(gl - g_max), 0.0)
    g_w = 1.0 / jnp.sum(g_exp, axis=-1, keepdims=True)
    g_sel = first_lane(gl == g_max)
    e_lo = N_EXPERT_GROUPS + g_sel * EXPERTS_PER_GROUP
    in_e = (lane >= e_lo) & (lane < e_lo + EXPERTS_PER_GROUP)
    el = jnp.where(in_e, logits, neg_inf)
    v1 = row_max(el)
    i1 = first_lane(el == v1)
    el2 = jnp.where(lane == i1, neg_inf, el)
    v2 = row_max(el2)
    i2 = first_lane(el2 == v2)
    e2 = jnp.exp(v2 - v1)
    w1 = g_w / (1.0 + e2)
    w2 = g_w * e2 / (1.0 + e2)
    comb = jnp.where(lane == i1, w1, jnp.where(lane == i2, w2, 0.0))
    return jnp.where(lane == GID_LANE, g_sel.astype(F32), comb)


def _merge_kernel(x_ref, o_ref, yr_ref, su_ref, yf_ref, yb_ref, mod_ref, g1_ref, g2_ref, *rest):
    gate_refs = rest[:N_GATE_BLK]
    (wa_ref, wr_ref, dsk_ref, wglu_ref, wout_ref, wrt_ref, brt_ref, x1_ref, h2_ref, comb_ref,
     h_scr, yr_scr, ys_scr, h2_scr) = rest[N_GATE_BLK:]
    tt = x_ref.shape[1]
    g1 = g1_ref[...]
    g2 = g2_ref[...]
    dsk = dsk_ref[...]
    per_gate = D_MODEL // GATE_BLK
    for b in range(BG):
        rb = slice(b * tt, (b + 1) * tt)
        h_scr[rb, :] = _modulate(x_ref[b], g1, mod_ref, b, 0, 1).astype(BF16)
        tm_rows = _strided_rows(b, tt)
        for c in range(N_SLAB):
            cs = slice(c * LANES, (c + 1) * LANES)
            yr_scr[rb, cs] = yr_ref.at[c][tm_rows, :].astype(BF16)
            ys = dsk[:, cs] * su_ref.at[c][tm_rows, :] + yf_ref.at[c][tm_rows, :] + yb_ref.at[c][tm_rows, :]
            ys_scr[rb, cs] = jax.nn.gelu(ys).astype(BF16)
    h = h_scr[...]
    y_attn = _dot(o_ref[...].reshape(BG * tt, ATTN_WIDTH).astype(BF16), wa_ref[...])
    y_rnn = _dot(yr_scr[...], wr_ref[...])
    glu = _dot(ys_scr[...], wglu_ref[...])
    y_ssm = glu[:, :D_MODEL] * _sigmoid(glu[:, D_MODEL:])
    mo = None
    for part in range(per_gate):
        cs = slice(part * GATE_BLK, (part + 1) * GATE_BLK)
        mix = None
        for j, y in enumerate((y_attn, y_rnn, y_ssm)):
            term = _sigmoid(_dot(h, gate_refs[j * per_gate + part][...])) * y[:, cs]
            mix = term if mix is None else mix + term
        contrib = _dot(mix.astype(BF16), wout_ref[cs, :])
        mo = contrib if mo is None else mo + contrib
    for b in range(BG):
        rb = slice(b * tt, (b + 1) * tt)
        x1 = x_ref[b] + mo[rb, :] * mod_ref[b, 2:3, :]
        x1_ref[b] = x1
        h2 = _modulate(x1, g2, mod_ref, b, 3, 4)
        h2_ref[b] = h2.astype(BF16)
        h2_scr[rb, :] = h2
    comb = _route(_dot3(h2_scr[...], wrt_ref[...]) + brt_ref[...])
    comb_ref[...] = comb.reshape(BG, tt, ROUTE_W)


def _merge(x, o, yr, su, yf, yb, mod, pp, l, tt=64):
    batch, seq, _ = x.shape
    nbg = batch // BG
    rows = BG * tt
    tok = lambda w: pl.BlockSpec((BG, tt, w), lambda g, i: (g, i, 0))
    slab = pl.BlockSpec((None, N_SLAB, rows, LANES), lambda g, i: (g, 0, i, 0))
    wspec = lambda *shape: _const(shape, (l,), single=True)
    gate_blk0 = PROJ_W // GATE_BLK
    gate = lambda j: pl.BlockSpec((None, D_MODEL, GATE_BLK), lambda *_: (l, 0, gate_blk0 + j),
                                  pipeline_mode=pl.Buffered(1))
    return pl.pallas_call(
        _merge_kernel,
        grid=(nbg, seq // tt),
        in_specs=[tok(D_MODEL), tok(ATTN_WIDTH), slab, slab, slab, slab,
                  _const((BG, N_MOD, D_MODEL)), _const((1, D_MODEL), (l,)), _const((1, D_MODEL), (l,))]
                 + [gate(j) for j in range(N_GATE_BLK)]
                 + [wspec(ATTN_WIDTH, D_MODEL), wspec(D_RNN, D_MODEL),
                  _const((1, D_SSM), (l,)), wspec(D_SSM, 2 * D_MODEL), wspec(D_MODEL, D_MODEL),
                  wspec(D_MODEL, ROUTE_W), _const((1, ROUTE_W), (l,))],
        out_specs=[tok(D_MODEL), tok(D_MODEL), tok(ROUTE_W)],
        out_shape=[jax.ShapeDtypeStruct((batch, seq, D_MODEL), F32), jax.ShapeDtypeStruct((batch, seq, D_MODEL), BF16),
                   jax.ShapeDtypeStruct((batch, seq, ROUTE_W), F32)],
        scratch_shapes=[pltpu.VMEM((rows, D_MODEL), BF16), pltpu.VMEM((rows, D_RNN), BF16),
                        pltpu.VMEM((rows, D_SSM), BF16), pltpu.VMEM((rows, D_MODEL), F32)],
        compiler_params=_cparams(("arbitrary", "arbitrary")),
        name="mixer_merge",
    )(x, o, yr, su, yf, yb, mod, pp['norm1'], pp['norm2'], *([pp['w_in']] * N_GATE_BLK), pp['w_attn_o'],
      pp['w_rnn_o'], pp['d_skip'], pp['w_glu'], pp['w_out'], pp['w_rt'], pp['b_rt'])


def _moe_kernel(x1_ref, h2_ref, comb_ref, mod_ref, wg_ref, wu_ref, wd_ref, *rest, final):
    if final:
        gf_ref, x2_ref, y_ref, hs_scr, cs_scr, acc_scr = rest
    else:
        x2_ref, hs_scr, cs_scr, acc_scr = rest
    t = MOE_T
    comb = comb_ref[...]
    lane = lax.broadcasted_iota(jnp.int32, (t, ROUTE_W), 1)
    gid = comb[:, GID_LANE:GID_LANE + 1]
    onehot = jnp.where((lane.astype(F32) == gid) & (lane < N_EXPERT_GROUPS), 1.0, 0.0)
    r_i = lax.broadcasted_iota(jnp.int32, (t, t), 0)
    c_i = lax.broadcasted_iota(jnp.int32, (t, t), 1)
    earlier = jnp.where(c_i < r_i, 1.0, 0.0).astype(BF16)
    csum = _dot(earlier, onehot.astype(BF16))
    rank = jnp.sum(onehot * csum, axis=-1, keepdims=True)
    counts = jnp.sum(onehot, axis=0, keepdims=True)
    lane_row = lax.broadcasted_iota(jnp.int32, (1, ROUTE_W), 1)
    cnt = [jnp.sum(jnp.where(lane_row == g, counts, 0.0)).astype(jnp.int32) for g in range(N_EXPERT_GROUPS)]
    start = [jnp.int32(0)]
    for g in range(1, N_EXPERT_GROUPS):
        start.append(start[-1] + cnt[g - 1])
    start_row = jnp.zeros((1, ROUTE_W), F32)
    for g in range(1, N_EXPERT_GROUPS):
        start_row = jnp.where(lane_row == g, start[g].astype(F32), start_row)
    pos = rank + jnp.sum(onehot * start_row, axis=-1, keepdims=True)
    pt = jnp.where(c_i.astype(F32) == pos, 1.0, 0.0).astype(BF16)
    hs_scr[0:t, :] = _dot_tn(pt, h2_ref[...]).astype(BF16)
    hs_scr[t:MOE_TP, :] = jnp.zeros((MOE_TP - t, D_MODEL), BF16)
    c_hi, c_lo = _split_bf16(comb)
    cs_scr[0:t, :] = _dot_tn(pt, c_hi) + _dot_tn(pt, c_lo)
    cs_scr[t:MOE_TP, :] = jnp.zeros((MOE_TP - t, ROUTE_W), F32)
    acc_scr[...] = jnp.zeros_like(acc_scr)
    lane_w = lax.broadcasted_iota(jnp.int32, (MOE_W, ROUTE_W), 1)

    def window(g, r0):
        hs = hs_scr[pl.ds(r0, MOE_W), :]
        cw = cs_scr[pl.ds(r0, MOE_W), :]
        out = None
        for j in range(EXPERTS_PER_GROUP):
            e = g * EXPERTS_PER_GROUP + j
            w_e = jnp.sum(jnp.where(lane_w == N_EXPERT_GROUPS + e, cw, 0.0), axis=-1, keepdims=True)
            gate = _dot(hs, wg_ref[e])
            act = gate * _sigmoid(gate) * _dot(hs, wu_ref[e]) * w_e
            o_e = _dot(act.astype(BF16), wd_ref[e])
            out = o_e if out is None else out + o_e
        acc_scr[pl.ds(r0, MOE_W), :] += out

    bases = [pl.multiple_of((start[g] // ROW_ALIGN) * ROW_ALIGN, ROW_ALIGN) for g in range(N_EXPERT_GROUPS)]
    for g in range(N_EXPERT_GROUPS):
        window(g, bases[g])
    for g in range(N_EXPERT_GROUPS):
        n_win = (start[g] - bases[g] + cnt[g] + MOE_W - 1) // MOE_W

        def more(k, carry, g=g):
            window(g, pl.multiple_of(bases[g] + k * MOE_W, ROW_ALIGN))
            return carry

        lax.fori_loop(1, n_win, more, 0)
    a_hi, a_lo = _split_bf16(acc_scr[0:t, :])
    moe = _dot(pt, a_hi) + _dot(pt, a_lo)
    x2 = x1_ref[...] + moe * mod_ref[5:6, :]
    x2_ref[...] = x2
    if final:
        y_ref[...] = _rms(x2, gf_ref[...])


def _moe(x1, h2, comb, mod, pp, l, seq, g_final):
    n = x1.shape[0]
    tm = MOE_T
    final = g_final is not None
    per_batch = mod.shape[0] > 1
    assert not per_batch or seq % tm == 0
    mod_idx = (lambda i: (i * tm // seq, 0, 0)) if per_batch else (lambda i: (0, 0, 0))
    rows = lambda w: pl.BlockSpec((tm, w), lambda i: (i, 0))
    in_specs = [rows(D_MODEL), rows(D_MODEL), rows(ROUTE_W), pl.BlockSpec((None, N_MOD, D_MODEL), mod_idx),
                _const((N_EXPERTS, D_MODEL, D_EXPERT), (l,), single=True),
                _const((N_EXPERTS, D_MODEL, D_EXPERT), (l,), single=True),
                _const((N_EXPERTS, D_EXPERT, D_MODEL), (l,), single=True)]
    args = [x1, h2, comb, mod, pp['w_gate'], pp['w_up'], pp['w_down']]
    out_specs = [rows(D_MODEL)]
    out_shape = [jax.ShapeDtypeStruct((n, D_MODEL), F32)]
    if final:
        in_specs.append(pl.BlockSpec((1, D_MODEL), lambda i: (0, 0)))
        args.append(g_final)
        out_specs.append(rows(D_MODEL))
        out_shape.append(jax.ShapeDtypeStruct((n, D_MODEL), F32))
    return pl.pallas_call(
        functools.partial(_moe_kernel, final=final),
        grid=(n // tm,),
        in_specs=in_specs,
        out_specs=out_specs,
        out_shape=out_shape,
        scratch_shapes=[pltpu.VMEM((MOE_TP, D_MODEL), BF16), pltpu.VMEM((MOE_TP, ROUTE_W), F32),
                        pltpu.VMEM((MOE_TP, D_MODEL), F32)],
        compiler_params=_cparams(("arbitrary",)),
        name="hier_moe",
    )(*args)


def _head_mean_matrix(width):
    idx = np.arange(width) // HEAD_DIM
    return jnp.asarray((idx[:, None] == idx[None, :]).astype(np.float32) / HEAD_DIM, dtype=BF16)


def _rope_tables(seq):
    t = np.arange(seq)
    row = (t // GRID_W).astype(np.float64)
    col = (t % GRID_W).astype(np.float64)
    inv = ROPE_BASE ** (-np.arange(ROPE_PAIRS_PER_AXIS, dtype=np.float64) / ROPE_PAIRS_PER_AXIS)
    inv = inv.astype(np.float32).astype(np.float64)
    ang = np.concatenate([row[:, None] * inv, col[:, None] * inv], axis=-1)
    ang = ang.astype(np.float32).astype(np.float64)
    cos = np.repeat(np.cos(ang), 2, axis=-1)
    sin = np.repeat(np.sin(ang), 2, axis=-1) * np.tile(np.array([-1.0, 1.0]), HEAD_DIM // 2)
    expand = lambda a: jnp.asarray(np.tile(a, (1, LANES // HEAD_DIM)), dtype=F32)
    return expand(cos), expand(sin)


def _prep_params(p, s5p):
    bf = lambda a: a.astype(BF16)
    pp = {}
    pp['w_in'] = bf(p['w_in'])
    pp['norm1'] = p['norm1'][:, None, :]
    pp['norm2'] = p['norm2'][:, None, :]
    pp['qg'] = jnp.tile(p['q_norm'], (1, N_HEADS))[:, None, :]
    pp['kg'] = jnp.tile(p['k_norm'], (1, N_KV_HEADS))[:, None, :]
    pp['s_q'] = _head_mean_matrix(ATTN_WIDTH)
    pp['s_k'] = _head_mean_matrix(KV_WIDTH)
    pp['w_attn_o'] = bf(p['w_attn_o'])
    pp['w_rnn_o'] = bf(p['w_rnn_o'])
    pp['w_glu'] = bf(p['w_glu'])
    pp['w_out'] = bf(p['w_out'])
    pp['d_skip'] = p['ssm_d'][:, None, :]
    pp['conv_w'] = p['conv_w']
    pp['conv_b'] = p['conv_b'][:, None, :]
    pp['lam'] = p['lru_lambda']
    per = LANES // RNN_BLOCK_W
    wa, wi = p['lru_w_a'], p['lru_w_i']
    w = jnp.stack([wa[:, 0], wi[:, 0], wa[:, 1], wi[:, 1]], axis=1)
    w = w.reshape(DEPTH, 4, N_SLAB, per, RNN_BLOCK_W, RNN_BLOCK_W)
    eye = jnp.eye(per, dtype=F32)
    wg = w.transpose(0, 2, 3, 4, 1, 5)[:, :, :, :, :, None, :] * eye[None, None, :, None, None, :, None]
    pp['wg'] = bf(wg.reshape(DEPTH, N_SLAB, LANES, 4 * LANES))
    ba, bi = p['lru_b_a'], p['lru_b_i']
    b = jnp.stack([ba[:, 0], bi[:, 0], ba[:, 1], bi[:, 1]], axis=1).reshape(DEPTH, 4, N_SLAB, LANES)
    pp['bg'] = b.transpose(0, 2, 1, 3).reshape(DEPTH, N_SLAB, 1, 4 * LANES)
    ab_re, ab_im, bb_re, bb_im = s5p
    eye_g = jnp.eye(SG_GROUPS, dtype=F32)
    lead = (DEPTH, N_DIR, SSM_SG, SG_GROUPS)
    bb = jnp.stack([bb_re, bb_im], axis=2).reshape(lead + (SSM_GROUP_W, 2, SSM_STATE))
    eye_g = bf(eye_g)
    bm = bf(bb)[:, :, :, :, :, :, None, :] * eye_g[None, None, None, :, None, None, :, None]
    pp['bm'] = bm.reshape(DEPTH, N_DIR, SSM_SG, LANES, 2 * SG_STATE)
    cc = jnp.stack([p['ssm_c_re'], -p['ssm_c_im']], axis=2)
    cc = cc.reshape(DEPTH, N_DIR, 2, SSM_SG, SG_GROUPS, SSM_GROUP_W, SSM_STATE).transpose(0, 1, 3, 2, 4, 6, 5)
    cm = bf(cc)[:, :, :, :, :, :, None, :] * eye_g[None, None, None, None, :, None, :, None]
    pp['cm'] = cm.reshape(DEPTH, N_DIR, SSM_SG, 2 * SG_STATE, LANES)
    are = ab_re.reshape(DEPTH, N_DIR, SSM_SG, SG_STATE)
    aim = ab_im.reshape(DEPTH, N_DIR, SSM_SG, SG_STATE)
    a_flat = jnp.concatenate([are, aim], axis=-1).reshape(DEPTH, N_DIR, 1, SSM_FLAT)
    pp['a_flat'] = jnp.broadcast_to(a_flat, (DEPTH, N_DIR, BG, SSM_FLAT))
    fill = ROUTE_W - N_EXPERT_GROUPS - N_EXPERTS
    pp['w_rt'] = jnp.concatenate([p['router_g_w'], p['router_e_w'], jnp.zeros((DEPTH, D_MODEL, fill), F32)], axis=-1)
    pp['b_rt'] = jnp.concatenate([p['router_g_b'], p['router_e_b'], jnp.zeros((DEPTH, fill), F32)], axis=-1)[:, None, :]
    pp['w_gate'] = bf(p['w_e_gate'])
    pp['w_up'] = bf(p['w_e_up'])
    pp['w_down'] = bf(p['w_e_down'])
    return pp


def _flat_state(re, im):
    b = re.shape[0]
    r = re.transpose(1, 2, 0, 3, 4).reshape(DEPTH, N_DIR, b, SSM_SG, SG_STATE)
    i = im.transpose(1, 2, 0, 3, 4).reshape(DEPTH, N_DIR, b, SSM_SG, SG_STATE)
    return jnp.concatenate([r, i], axis=-1).reshape(DEPTH, N_DIR, b, SSM_FLAT)


def _unflat_state(flat):
    b = flat.shape[2]
    f = flat.reshape(DEPTH, N_DIR, b, SSM_SG, 2, SG_GROUPS, SSM_STATE).transpose(4, 2, 0, 1, 3, 5, 6)
    f = f.reshape(2, b, DEPTH, N_DIR, SSM_GROUPS, SSM_STATE)
    return f[0], f[1]


def _trunk_layer(x, mod, mod_moe, pp, l, ctx, rope_tabs, g_final, attn_tq):
    batch, seq, _ = x.shape
    q, k, v, rx, rg, su = _in_proj(x, mod, pp, l, rope_tabs)
    if ctx is None:
        attn_ctx = None
        h0 = jnp.zeros((N_DIR, batch, D_RNN), F32)
        s0 = jnp.zeros((N_DIR, batch, SSM_FLAT), F32)
    else:
        cache_k, cache_v, h0, s0 = ctx
        attn_ctx = (cache_k, cache_v, l)
    o = _attention(q, k, v, attn_ctx, attn_tq)
    yr, rnn_fin = _rglru(rx, rg, pp, l, h0)
    yf, yb, ssm_fin = _s5(su, pp, l, s0)
    x1, h2, comb = _merge(x, o, yr, su, yf, yb, mod, pp, l)
    flat = lambda a: a.reshape(batch * seq, a.shape[-1])
    outs = _moe(flat(x1), flat(h2), flat(comb), mod_moe, pp, l, seq, g_final)
    return [a.reshape(batch, seq, D_MODEL) for a in outs], (k, v, rnn_fin, ssm_fin)


def kernel(x_prompt, x_sample, cache_k, cache_v, state_rglru, state_ssm_re, state_ssm_im, c, c_ctx, w_mod, b_mod, norm1, norm2, w_in, q_norm, k_norm, w_attn_o, conv_w, conv_b, lru_w_a, lru_b_a, lru_w_i, lru_b_i, lru_lambda, w_rnn_o, ssm_lam_re, ssm_lam_im, ssm_log_step, ssm_b_re, ssm_b_im, ssm_c_re, ssm_c_im, ssm_d, w_glu, w_out, router_g_w, router_g_b, router_e_w, router_e_b, w_e_gate, w_e_up, w_e_down, final_norm):
    p = dict(norm1=norm1, norm2=norm2, w_in=w_in, q_norm=q_norm, k_norm=k_norm, w_attn_o=w_attn_o, conv_w=conv_w,
             conv_b=conv_b, lru_w_a=lru_w_a, lru_b_a=lru_b_a, lru_w_i=lru_w_i, lru_b_i=lru_b_i,
             lru_lambda=lru_lambda, w_rnn_o=w_rnn_o, ssm_c_re=ssm_c_re, ssm_c_im=ssm_c_im, ssm_d=ssm_d, w_glu=w_glu,
             w_out=w_out, router_g_w=router_g_w, router_g_b=router_g_b, router_e_w=router_e_w,
             router_e_b=router_e_b, w_e_gate=w_e_gate, w_e_up=w_e_up, w_e_down=w_e_down)
    bc, lc, _ = x_prompt.shape
    bd, ld, _ = x_sample.shape
    past = cache_k.shape[2]
    assert bd == BG and bc % BG == 0

    n_cond = 1 + bd
    cond_rows = -(-n_cond // SUBLANES) * SUBLANES
    cond = jnp.concatenate([c_ctx[None], c, jnp.zeros((cond_rows - n_cond, D_MODEL), F32)], axis=0)
    mods = _modulation(cond, w_mod, b_mod).reshape(DEPTH, cond_rows, N_MOD, D_MODEL)
    s5p = _s5_params(ssm_lam_re, ssm_lam_im, ssm_log_step, ssm_b_re, ssm_b_im)
    pp = _prep_params(p, s5p)
    rope_tabs = _rope_tables(ld)
    final_g = final_norm[None]

    ck = cache_k.reshape(bd, DEPTH, past, KV_WIDTH)
    cv = cache_v.reshape(bd, DEPTH, past, KV_WIDTH)
    h0_lat = state_rglru.transpose(1, 2, 0, 3)
    s0_lat = _flat_state(state_ssm_re, state_ssm_im)
    x_ctx, x_lat = x_prompt, x_sample
    ks, vs, rs, ss = [], [], [], []
    y_ctx = y_lat = None
    for l in range(DEPTH):
        g_final = final_g if l == DEPTH - 1 else None
        mod_ctx1 = mods[l, 0:1]
        mod_ctx = jnp.broadcast_to(mod_ctx1, (BG, N_MOD, D_MODEL))
        mod_lat = mods[l, 1:n_cond]
        outs, (k_l, v_l, r_l, s_l) = _trunk_layer(x_ctx, mod_ctx, mod_ctx1, pp, l, None, None, g_final, lc)
        x_ctx = outs[0]
        if g_final is not None:
            y_ctx = outs[1]
        ks.append(k_l)
        vs.append(v_l)
        rs.append(r_l)
        ss.append(s_l)
        ctx = (ck, cv, h0_lat[l], s0_lat[l])
        outs, _ = _trunk_layer(x_lat, mod_lat, mod_lat, pp, l, ctx, rope_tabs, g_final, 512)
        x_lat = outs[0]
        if g_final is not None:
            y_lat = outs[1]
    new_k = jnp.stack(ks, axis=1).reshape(bc, DEPTH, lc, N_KV_HEADS, HEAD_DIM)
    new_v = jnp.stack(vs, axis=1).reshape(bc, DEPTH, lc, N_KV_HEADS, HEAD_DIM)
    new_r = jnp.stack(rs, axis=0).transpose(2, 0, 1, 3)
    new_re, new_im = _unflat_state(jnp.stack(ss, axis=0))
    return (y_ctx, y_lat, new_k, new_v, new_r, new_re, new_im)
```

```python
import functools

import numpy as np
import jax
import jax.numpy as jnp
from jax import lax
from jax.experimental import pallas as pl
from jax.experimental.pallas import tpu as pltpu

F32 = jnp.float32
BF16 = jnp.bfloat16

D_MODEL = 1024
DEPTH = 2
GRID_W = 64
N_DIR = 2
NORM_EPS = 1e-6
N_MOD = 6

N_HEADS = 8
N_KV_HEADS = 4
HEAD_DIM = 64
ATTN_WIDTH = N_HEADS * HEAD_DIM
KV_WIDTH = N_KV_HEADS * HEAD_DIM
ROPE_BASE = 10000.0
ROPE_PAIRS_PER_AXIS = HEAD_DIM // 4

D_RNN = D_MODEL // 2
RNN_BLOCKS = 8
RNN_BLOCK_W = D_RNN // RNN_BLOCKS
CONV_W = 4
CONV_LEFT = 2
LRU_C = 8.0

D_SSM = D_MODEL // 2
SSM_GROUP_W = 16
SSM_GROUPS = D_SSM // SSM_GROUP_W
SSM_STATE = 64

N_EXPERT_GROUPS = 4
EXPERTS_PER_GROUP = 4
N_EXPERTS = N_EXPERT_GROUPS * EXPERTS_PER_GROUP
D_EXPERT = D_MODEL // 4

LANES = 128
SUBLANES = 8
VMEM_LIMIT = 56 * 1024 * 1024

BG = SUBLANES
PROJ_W = ATTN_WIDTH + 2 * KV_WIDTH + 2 * D_RNN + D_SSM
N_SLAB = D_RNN // LANES
GATE_BLK = 512
N_GATE_BLK = 3 * D_MODEL // GATE_BLK
assert PROJ_W % GATE_BLK == 0

SG_GROUPS = LANES // SSM_GROUP_W
SSM_SG = SSM_GROUPS // SG_GROUPS
SG_STATE = SG_GROUPS * SSM_STATE
SSM_FLAT = SSM_SG * 2 * SG_STATE

ROUTE_W = LANES
GID_LANE = ROUTE_W - 1
MOE_T = 512
MOE_W = 160
ROW_ALIGN = 16
MOE_TP = 768
assert MOE_TP >= MOE_T + MOE_W + ROW_ALIGN and MOE_TP % LANES == 0 and MOE_W % ROW_ALIGN == 0


def _cparams(sem, vmem=VMEM_LIMIT):
    return pltpu.CompilerParams(dimension_semantics=sem, vmem_limit_bytes=vmem)


def _const(shape, lead=(), single=False):
    idx = tuple(lead) + (0,) * len(shape)
    block = (None,) * len(lead) + tuple(shape)
    if single:
        return pl.BlockSpec(block, lambda *_: idx, pipeline_mode=pl.Buffered(1))
    return pl.BlockSpec(block, lambda *_: idx)


def _dot(a, b):
    return jnp.dot(a, b, preferred_element_type=F32)


def _dot_nt(a, b):
    return lax.dot_general(a, b, (((1,), (1,)), ((), ())), preferred_element_type=F32)


def _dot_tn(a, b):
    return lax.dot_general(a, b, (((0,), (0,)), ((), ())), preferred_element_type=F32)


def _split_bf16(a):
    hi = a.astype(BF16)
    lo = (a - hi.astype(F32)).astype(BF16)
    return hi, lo


def _dot3(a, b):
    a_hi, a_lo = _split_bf16(a)
    b_hi, b_lo = _split_bf16(b)
    return _dot(a_hi, b_hi) + (_dot(a_lo, b_hi) + _dot(a_hi, b_lo))


def _sigmoid(x):
    return 0.5 * jnp.tanh(0.5 * x) + 0.5


def _rms(x, g):
    ms = jnp.mean(x * x, axis=-1, keepdims=True)
    return x * lax.rsqrt(ms + NORM_EPS) * g


def _modulate(x, g, mod_ref, b, shift_i, scale_i):
    return _rms(x, g) * (1.0 + mod_ref[b, scale_i:scale_i + 1, :]) + mod_ref[b, shift_i:shift_i + 1, :]


def _strided_rows(b, n):
    return pl.ds(b, n, stride=BG)


def _mod_kernel(c_ref, w_ref, b_ref, o_ref):
    c = c_ref[...]
    a = c * _sigmoid(c)
    o_ref[...] = _dot3(a, w_ref[...]) + b_ref[...]


def _modulation(cond, w_mod, b_mod):
    rows = cond.shape[0]
    tn = 1536
    width = N_MOD * D_MODEL
    return pl.pallas_call(
        _mod_kernel,
        grid=(DEPTH, width // tn),
        in_specs=[
            pl.BlockSpec((rows, D_MODEL), lambda l, j: (0, 0)),
            pl.BlockSpec((None, D_MODEL, tn), lambda l, j: (l, 0, j)),
            pl.BlockSpec((None, 1, tn), lambda l, j: (l, 0, j)),
        ],
        out_specs=pl.BlockSpec((None, rows, tn), lambda l, j: (l, 0, j)),
        out_shape=jax.ShapeDtypeStruct((DEPTH, rows, width), F32),
        compiler_params=_cparams(("arbitrary", "arbitrary")),
        name="adaln_modulation",
    )(cond, w_mod, b_mod.reshape(DEPTH, 1, width))


def _head_norm(z, s_ref, gain):
    ms = _dot((z * z).astype(BF16), s_ref[...])
    return z * lax.rsqrt(ms + NORM_EPS) * gain


def _rope(z, cos, sin):
    rows = z.shape[0]
    tt = cos.shape[0]
    lane = lax.broadcasted_iota(jnp.int32, (rows, LANES), 1)
    partner = jnp.where((lane & 1) == 0, pltpu.roll(z, LANES - 1, 1), pltpu.roll(z, 1, 1))
    z3 = z.reshape(rows // tt, tt, LANES)
    p3 = partner.reshape(rows // tt, tt, LANES)
    return z3 * cos[None] + p3 * sin[None]


def _in_proj_kernel(x_ref, mod_ref, g_ref, w_ref, qg_ref, kg_ref, sq_ref, sk_ref, *rest, rope):
    if rope:
        cos_ref, sin_ref, q_ref, k_ref, v_ref, rx_ref, rg_ref, su_ref, h_scr = rest
    else:
        q_ref, k_ref, v_ref, rx_ref, rg_ref, su_ref, h_scr = rest
    tt = x_ref.shape[1]
    g = g_ref[...]
    qscale = HEAD_DIM ** -0.5
    for b in range(BG):
        h_scr[b * tt:(b + 1) * tt, :] = _modulate(x_ref[b], g, mod_ref, b, 0, 1).astype(BF16)
    h = h_scr[...]

    def proj(lo, width):
        return _dot(h, w_ref[:, lo:lo + width])

    q = _head_norm(proj(0, ATTN_WIDTH), sq_ref, qg_ref[...]) * qscale
    k = _head_norm(proj(ATTN_WIDTH, KV_WIDTH), sk_ref, kg_ref[...])
    if rope:
        cos = cos_ref[...]
        sin = sin_ref[...]
        for dst, z in ((q_ref, q), (k_ref, k)):
            for c in range(z.shape[1] // LANES):
                cs = slice(c * LANES, (c + 1) * LANES)
                dst[:, :, cs] = _rope(z[:, cs], cos, sin)
    else:
        q_ref[...] = q.reshape(BG, tt, ATTN_WIDTH)
        k_ref[...] = k.reshape(BG, tt, KV_WIDTH)
    off = ATTN_WIDTH + KV_WIDTH
    v_ref[...] = proj(off, KV_WIDTH).reshape(BG, tt, KV_WIDTH)
    off += KV_WIDTH
    for dst in (rx_ref, rg_ref, su_ref):
        z = proj(off, D_RNN)
        off += D_RNN
        for b in range(BG):
            for c in range(N_SLAB):
                dst.at[c][_strided_rows(b, tt), :] = z[b * tt:(b + 1) * tt, c * LANES:(c + 1) * LANES]


def _in_proj(x, mod, pp, l, rope_tabs, tt=64):
    batch, seq, _ = x.shape
    nbg = batch // BG
    rope = rope_tabs is not None
    tok = lambda w: pl.BlockSpec((BG, tt, w), lambda g, i: (g, i, 0))
    slab = pl.BlockSpec((None, N_SLAB, tt * BG, LANES), lambda g, i: (g, 0, i, 0))
    in_specs = [tok(D_MODEL), _const((BG, N_MOD, D_MODEL)), _const((1, D_MODEL), (l,)),
                pl.BlockSpec((None, D_MODEL, PROJ_W), lambda *_: (l, 0, 0), pipeline_mode=pl.Buffered(1)),
                _const((1, ATTN_WIDTH), (l,)),
                _const((1, KV_WIDTH), (l,)), _const((ATTN_WIDTH, ATTN_WIDTH)), _const((KV_WIDTH, KV_WIDTH))]
    args = [x, mod, pp['norm1'], pp['w_in'], pp['qg'], pp['kg'], pp['s_q'], pp['s_k']]
    if rope:
        tab = pl.BlockSpec((tt, LANES), lambda g, i: (i, 0))
        in_specs += [tab, tab]
        args += list(rope_tabs)
    tok_shape = lambda w: jax.ShapeDtypeStruct((batch, seq, w), F32)
    slab_shape = jax.ShapeDtypeStruct((nbg, N_SLAB, seq * BG, LANES), F32)
    return pl.pallas_call(
        functools.partial(_in_proj_kernel, rope=rope),
        grid=(nbg, seq // tt),
        in_specs=in_specs,
        out_specs=[tok(ATTN_WIDTH), tok(KV_WIDTH), tok(KV_WIDTH), slab, slab, slab],
        out_shape=[tok_shape(ATTN_WIDTH), tok_shape(KV_WIDTH), tok_shape(KV_WIDTH), slab_shape, slab_shape, slab_shape],
        scratch_shapes=[pltpu.VMEM((BG * tt, D_MODEL), BF16)],
        compiler_params=_cparams(("arbitrary", "arbitrary")),
        name="mixer_in_proj",
    )(*args)


def _attn_kernel(q_ref, k_ref, v_ref, *rest, n_ctx):
    if n_ctx:
        kc_ref, vc_ref, o_ref, kd_ref, va_ref, vb_ref = rest
    else:
        o_ref, kd_ref, va_ref, vb_ref = rest
    lk = k_ref.shape[0]
    tq = q_ref.shape[0]

    def lane_lt(rows):
        return lax.broadcasted_iota(jnp.int32, (rows, LANES), 1) < HEAD_DIM

    @pl.when(pl.program_id(1) == 0)
    def _():
        def fill(src_k, src_v, base, rows):
            low = lane_lt(rows)
            for pair in range(N_KV_HEADS // 2):
                sl = slice(pair * LANES, (pair + 1) * LANES)
                k2 = src_k[:, sl]
                v2 = src_v[:, sl]
                k2r = pltpu.roll(k2, HEAD_DIM, 1)
                v2r = pltpu.roll(v2, HEAD_DIM, 1)
                for odd in range(2):
                    j = 2 * pair + odd
                    kd = jnp.where(low, k2r, k2) if odd else jnp.where(low, k2, k2r)
                    vd = jnp.where(low, v2r, v2) if odd else jnp.where(low, v2, v2r)
                    kd_ref[j, base:base + rows, :] = kd.astype(BF16)
                    va_ref[j, base:base + rows, :] = jnp.where(low, vd, 0.0).astype(BF16)
                    vb_ref[j, base:base + rows, :] = jnp.where(low, 0.0, vd).astype(BF16)

        if n_ctx:
            fill(kc_ref, vc_ref, 0, n_ctx)
        fill(k_ref, v_ref, n_ctx, lk)

    low = lane_lt(tq)
    for j in range(N_KV_HEADS):
        q2 = q_ref[:, j * LANES:(j + 1) * LANES]
        kd = kd_ref[j]
        s0 = _dot_nt(jnp.where(low, q2, 0.0).astype(BF16), kd)
        s1 = _dot_nt(jnp.where(low, 0.0, q2).astype(BF16), kd)
        p0 = jnp.exp(s0 - jnp.max(s0, axis=-1, keepdims=True))
        p1 = jnp.exp(s1 - jnp.max(s1, axis=-1, keepdims=True))
        r0 = 1.0 / jnp.sum(p0, axis=-1, keepdims=True)
        r1 = 1.0 / jnp.sum(p1, axis=-1, keepdims=True)
        o2 = _dot(p0.astype(BF16), va_ref[j]) + _dot(p1.astype(BF16), vb_ref[j])
        o_ref[:, j * LANES:(j + 1) * LANES] = o2 * jnp.where(low, r0, r1)


def _attention(q, k, v, ctx, tq):
    batch, seq, _ = q.shape
    n_ctx = 0 if ctx is None else ctx[0].shape[2]
    tok = lambda w: pl.BlockSpec((None, seq, w), lambda b, i: (b, 0, 0))
    in_specs = [pl.BlockSpec((None, tq, ATTN_WIDTH), lambda b, i: (b, i, 0)), tok(KV_WIDTH), tok(KV_WIDTH)]
    args = [q, k, v]
    if ctx is not None:
        cache_k, cache_v, layer = ctx
        cspec = pl.BlockSpec((None, None, n_ctx, KV_WIDTH), lambda b, i: (b, layer, 0, 0))
        in_specs += [cspec, cspec]
        args += [cache_k, cache_v]
    lt = n_ctx + seq
    return pl.pallas_call(
        functools.partial(_attn_kernel, n_ctx=n_ctx),
        grid=(batch, seq // tq),
        in_specs=in_specs,
        out_specs=pl.BlockSpec((None, tq, ATTN_WIDTH), lambda b, i: (b, i, 0)),
        out_shape=jax.ShapeDtypeStruct((batch, seq, ATTN_WIDTH), F32),
        scratch_shapes=[pltpu.VMEM((N_KV_HEADS, lt, LANES), BF16)] * 3,
        compiler_params=_cparams(("arbitrary", "arbitrary")),
        name="gqa_attention",
    )(*args)


def _rglru_kernel(rx_ref, rg_ref, cw_ref, cb_ref, wg_ref, bg_ref, lam_ref, h0_ref, y_ref, fin_ref,
                  xpad_ref, af_ref, uf_ref, ab_ref, ub_ref, *, seq, chunk):
    n = seq * BG
    pad = CONV_LEFT * BG
    n_chunks = n // chunk
    zeros = jnp.zeros((pad, LANES), F32)
    xpad_ref[0:pad, :] = zeros
    xpad_ref[pad + n:pad + n + pad, :] = zeros

    def copy_in(i, c):
        r0 = pl.multiple_of(i * chunk, chunk)
        xpad_ref[pl.ds(pad + r0, chunk), :] = rx_ref[pl.ds(r0, chunk), :]
        return c

    lax.fori_loop(0, n_chunks, copy_in, 0)

    neg = -lam_ref[...]
    softplus = jnp.maximum(neg, 0.0) + jnp.log1p(jnp.exp(-jnp.abs(neg)))
    decay = -LRU_C * softplus
    cw = cw_ref[...]
    cb = cb_ref[...]

    def gates(i, c):
        r0 = pl.multiple_of(i * chunk, chunk)
        xc = cb
        for j in range(CONV_W):
            xc = xc + xpad_ref[pl.ds(r0 + j * BG, chunk), :] * cw[j:j + 1, :]
        g = _dot(xc.astype(BF16), wg_ref[...]) + bg_ref[...]
        for d, (a_ref, u_ref) in enumerate(((af_ref, uf_ref), (ab_ref, ub_ref))):
            r = _sigmoid(g[:, (2 * d) * LANES:(2 * d + 1) * LANES])
            gi = _sigmoid(g[:, (2 * d + 1) * LANES:(2 * d + 2) * LANES])
            log_a = r * decay[d:d + 1, :]
            a_ref[pl.ds(r0, chunk), :] = jnp.exp(log_a)
            th = jnp.tanh(log_a)
            w = -2.0 * th
            mult = jnp.where(w > 0.0, w * lax.rsqrt(w * (1.0 - th)), 0.0)
            u_ref[pl.ds(r0, chunk), :] = mult * (gi * xc)
        return c

    lax.fori_loop(0, n_chunks, gates, 0)

    def step(t, carry):
        hf, hb = carry
        rf = pl.multiple_of(t * BG, BG)
        rb = pl.multiple_of((seq - 1 - t) * BG, BG)
        hf = af_ref[pl.ds(rf, BG), :] * hf + uf_ref[pl.ds(rf, BG), :]
        y_ref[pl.ds(rf, BG), :] = hf
        hb = ab_ref[pl.ds(rb, BG), :] * hb + ub_ref[pl.ds(rb, BG), :]
        xpad_ref[pl.ds(rb, BG), :] = hb
        return hf, hb

    hf, hb = lax.fori_loop(0, seq, step, (h0_ref[0], h0_ref[1]), unroll=8)
    fin_ref[0] = hf
    fin_ref[1] = hb

    def emit(i, c):
        r0 = pl.multiple_of(i * chunk, chunk)
        gate = jax.nn.gelu(rg_ref[pl.ds(r0, chunk), :])
        y_ref[pl.ds(r0, chunk), :] = gate * (y_ref[pl.ds(r0, chunk), :] + xpad_ref[pl.ds(r0, chunk), :])
        return c

    lax.fori_loop(0, n_chunks, emit, 0)


def _rglru(rx, rg, pp, l, h0, chunk=512):
    nbg, _, n, _ = rx.shape
    seq = n // BG
    slab = pl.BlockSpec((None, None, n, LANES), lambda g, c: (g, c, 0, 0))
    col = lambda rows: pl.BlockSpec((None, rows, LANES), lambda g, c: (l, 0, c))
    state = pl.BlockSpec((N_DIR, BG, LANES), lambda g, c: (0, g, c))
    return pl.pallas_call(
        functools.partial(_rglru_kernel, seq=seq, chunk=chunk),
        grid=(nbg, N_SLAB),
        in_specs=[slab, slab, col(CONV_W), col(1),
                  pl.BlockSpec((None, None, LANES, 4 * LANES), lambda g, c: (l, c, 0, 0)),
                  pl.BlockSpec((None, None, 1, 4 * LANES), lambda g, c: (l, c, 0, 0)),
                  col(N_DIR), state],
        out_specs=[slab, state],
        out_shape=[jax.ShapeDtypeStruct(rx.shape, F32), jax.ShapeDtypeStruct((N_DIR, nbg * BG, D_RNN), F32)],
        scratch_shapes=[pltpu.VMEM((n + 2 * CONV_LEFT * BG, LANES), F32)] + [pltpu.VMEM((n, LANES), F32)] * 4,
        compiler_params=_cparams(("arbitrary", "arbitrary")),
        name="rglru_scan",
    )(rx, rg, pp['conv_w'], pp['conv_b'], pp['wg'], pp['bg'], pp['lam'], h0)


def _s5_param_kernel(lr_ref, li_ref, ls_ref, br_ref, bi_ref, are_ref, aim_ref, bbr_ref, bbi_ref):
    lr = lr_ref[...]
    li = li_ref[...]
    step = jnp.exp(ls_ref[...])
    mag = jnp.exp(lr * step)
    ab_re = mag * jnp.cos(li * step)
    ab_im = mag * jnp.sin(li * step)
    den = lr * lr + li * li
    nr = ab_re - 1.0
    ni = ab_im
    f_re = (nr * lr + ni * li) / den
    f_im = (ni * lr - nr * li) / den
    are_ref[...] = ab_re
    aim_ref[...] = ab_im
    br = br_ref[...]
    bi = bi_ref[...]
    bbr_ref[...] = f_re[:, None, :] * br - f_im[:, None, :] * bi
    bbi_ref[...] = f_re[:, None, :] * bi + f_im[:, None, :] * br


def _s5_params(lam_re, lam_im, log_step, b_re, b_im):
    rows = DEPTH * N_DIR * SSM_GROUPS
    flat = lambda a: a.reshape(rows, SSM_STATE)
    bt = lambda a: a.reshape(rows, SSM_STATE, SSM_GROUP_W).transpose(0, 2, 1)
    vec = jax.ShapeDtypeStruct((rows, SSM_STATE), F32)
    mat = jax.ShapeDtypeStruct((rows, SSM_GROUP_W, SSM_STATE), F32)
    return pl.pallas_call(
        _s5_param_kernel,
        out_shape=[vec, vec, mat, mat],
        name="s5_discretise",
    )(flat(lam_re), flat(lam_im), log_step.reshape(rows, 1), bt(b_re), bt(b_im))


def _s5_kernel(uf_ref, ub_ref, bm_ref, cm_ref, a_ref, s0_ref, yf_ref, yb_ref, fin_ref, bu_ref, xs_ref, st_ref):
    rows = uf_ref.shape[1]
    steps = rows // BG
    i = pl.program_id(1)

    @pl.when(i == 0)
    def _():
        st_ref[...] = s0_ref[...]

    for d, u_ref in enumerate((uf_ref, ub_ref)):
        for s in range(SSM_SG):
            bu_ref[d, :, s * 2 * SG_STATE:(s + 1) * 2 * SG_STATE] = _dot(u_ref[s].astype(BF16), bm_ref[d, s])

    pair_cols = [(s * 2 * SG_STATE + j * LANES, s * 2 * SG_STATE + SG_STATE + j * LANES)
                 for s in range(SSM_SG) for j in range(SG_STATE // LANES)]
    for d in range(N_DIR):
        state = [(st_ref[d, :, cr:cr + LANES], st_ref[d, :, ci:ci + LANES]) for cr, ci in pair_cols]
        for t in (range(steps) if d == 0 else range(steps - 1, -1, -1)):
            r0 = t * BG
            for p, (cr, ci) in enumerate(pair_cols):
                xr, xi = state[p]
                ar = a_ref[d, :, cr:cr + LANES]
                ai = a_ref[d, :, ci:ci + LANES]
                nr = ar * xr - ai * xi + bu_ref[d, r0:r0 + BG, cr:cr + LANES]
                ni = ar * xi + ai * xr + bu_ref[d, r0:r0 + BG, ci:ci + LANES]
                xs_ref[d, r0:r0 + BG, cr:cr + LANES] = nr
                xs_ref[d, r0:r0 + BG, ci:ci + LANES] = ni
                state[p] = (nr, ni)
        for (cr, ci), (xr, xi) in zip(pair_cols, state):
            st_ref[d, :, cr:cr + LANES] = xr
            st_ref[d, :, ci:ci + LANES] = xi

    for d, y_ref in enumerate((yf_ref, yb_ref)):
        for s in range(SSM_SG):
            xs = xs_ref[d, :, s * 2 * SG_STATE:(s + 1) * 2 * SG_STATE].astype(BF16)
            y_ref[s] = _dot(xs, cm_ref[d, s])

    @pl.when(i == pl.num_programs(1) - 1)
    def _():
        fin_ref[...] = st_ref[...]


def _s5(su, pp, l, s0, rows=512):
    nbg, _, n, _ = su.shape
    nt = n // rows
    fwd = pl.BlockSpec((None, N_SLAB, rows, LANES), lambda g, i: (g, 0, i, 0))
    bwd = pl.BlockSpec((None, N_SLAB, rows, LANES), lambda g, i: (g, 0, nt - 1 - i, 0))
    st = pl.BlockSpec((N_DIR, BG, SSM_FLAT), lambda g, i: (0, g, 0))
    return pl.pallas_call(
        _s5_kernel,
        grid=(nbg, nt),
        in_specs=[fwd, bwd, _const((N_DIR, SSM_SG, LANES, 2 * SG_STATE), (l,), single=True),
                  _const((N_DIR, SSM_SG, 2 * SG_STATE, LANES), (l,), single=True),
                  _const((N_DIR, BG, SSM_FLAT), (l,)), st],
        out_specs=[fwd, bwd, st],
        out_shape=[jax.ShapeDtypeStruct(su.shape, F32)] * 2 + [jax.ShapeDtypeStruct((N_DIR, nbg * BG, SSM_FLAT), F32)],
        scratch_shapes=[pltpu.VMEM((N_DIR, rows, SSM_FLAT), F32), pltpu.VMEM((N_DIR, rows, SSM_FLAT), F32),
                        pltpu.VMEM((N_DIR, BG, SSM_FLAT), F32)],
        compiler_params=_cparams(("arbitrary", "arbitrary")),
        name="s5_scan",
    )(su, su, pp['bm'], pp['cm'], pp['a_flat'], s0)


def _route(logits):
    rows = logits.shape[0]
    lane = lax.broadcasted_iota(jnp.int32, (rows, ROUTE_W), 1)
    neg_inf = -jnp.inf
    big = ROUTE_W

    def row_max(v):
        return jnp.max(v, axis=-1, keepdims=True)

    def first_lane(mask):
        return jnp.min(jnp.where(mask, lane, big), axis=-1, keepdims=True)

    in_g = lane < N_EXPERT_GROUPS
    gl = jnp.where(in_g, logits, neg_inf)
    g_max = row_max(gl)
    g_exp = jnp.where(in_g, jnp.exp(gl - g_max), 0.0)
    g_w = 1.0 / jnp.sum(g_exp, axis=-1, keepdims=True)
    g_sel = first_lane(gl == g_max)
    e_lo = N_EXPERT_GROUPS + g_sel * EXPERTS_PER_GROUP
    in_e = (lane >= e_lo) & (lane < e_lo + EXPERTS_PER_GROUP)
    el = jnp.where(in_e, logits, neg_inf)
    v1 = row_max(el)
    i1 = first_lane(el == v1)
    el2 = jnp.where(lane == i1, neg_inf, el)
    v2 = row_max(el2)
    i2 = first_lane(el2 == v2)
    e2 = jnp.exp(v2 - v1)
    w1 = g_w / (1.0 + e2)
    w2 = g_w * e2 / (1.0 + e2)
    comb = jnp.where(lane == i1, w1, jnp.where(lane == i2, w2, 0.0))
    return jnp.where(lane == GID_LANE, g_sel.astype(F32), comb)


def _merge_kernel(x_ref, o_ref, yr_ref, su_ref, yf_ref, yb_ref, mod_ref, g1_ref, g2_ref, *rest):
    gate_refs = rest[:N_GATE_BLK]
    (wa_ref, wr_ref, dsk_ref, wglu_ref, wout_ref, wrt_ref, brt_ref, x1_ref, h2_ref, comb_ref,
     h_scr, yr_scr, ys_scr, h2_scr) = rest[N_GATE_BLK:]
    tt = x_ref.shape[1]
    g1 = g1_ref[...]
    g2 = g2_ref[...]
    dsk = dsk_ref[...]
    per_gate = D_MODEL // GATE_BLK
    for b in range(BG):
        rb = slice(b * tt, (b + 1) * tt)
        h_scr[rb, :] = _modulate(x_ref[b], g1, mod_ref, b, 0, 1).astype(BF16)
        tm_rows = _strided_rows(b, tt)
        for c in range(N_SLAB):
            cs = slice(c * LANES, (c + 1) * LANES)
            yr_scr[rb, cs] = yr_ref.at[c][tm_rows, :].astype(BF16)
            ys = dsk[:, cs] * su_ref.at[c][tm_rows, :] + yf_ref.at[c][tm_rows, :] + yb_ref.at[c][tm_rows, :]
            ys_scr[rb, cs] = jax.nn.gelu(ys).astype(BF16)
    h = h_scr[...]
    y_attn = _dot(o_ref[...].reshape(BG * tt, ATTN_WIDTH).astype(BF16), wa_ref[...])
    y_rnn = _dot(yr_scr[...], wr_ref[...])
    glu = _dot(ys_scr[...], wglu_ref[...])
    y_ssm = glu[:, :D_MODEL] * _sigmoid(glu[:, D_MODEL:])
    mo = None
    for part in range(per_gate):
        cs = slice(part * GATE_BLK, (part + 1) * GATE_BLK)
        mix = None
        for j, y in enumerate((y_attn, y_rnn, y_ssm)):
            term = _sigmoid(_dot(h, gate_refs[j * per_gate + part][...])) * y[:, cs]
            mix = term if mix is None else mix + term
        contrib = _dot(mix.astype(BF16), wout_ref[cs, :])
        mo = contrib if mo is None else mo + contrib
    for b in range(BG):
        rb = slice(b * tt, (b + 1) * tt)
        x1 = x_ref[b] + mo[rb, :] * mod_ref[b, 2:3, :]
        x1_ref[b] = x1
        h2 = _modulate(x1, g2, mod_ref, b, 3, 4)
        h2_ref[b] = h2.astype(BF16)
        h2_scr[rb, :] = h2
    comb = _route(_dot3(h2_scr[...], wrt_ref[...]) + brt_ref[...])
    comb_ref[...] = comb.reshape(BG, tt, ROUTE_W)


def _merge(x, o, yr, su, yf, yb, mod, pp, l, tt=64):
    batch, seq, _ = x.shape
    nbg = batch // BG
    rows = BG * tt
    tok = lambda w: pl.BlockSpec((BG, tt, w), lambda g, i: (g, i, 0))
    slab = pl.BlockSpec((None, N_SLAB, rows, LANES), lambda g, i: (g, 0, i, 0))
    wspec = lambda *shape: _const(shape, (l,), single=True)
    gate_blk0 = PROJ_W // GATE_BLK
    gate = lambda j: pl.BlockSpec((None, D_MODEL, GATE_BLK), lambda *_: (l, 0, gate_blk0 + j),
                                  pipeline_mode=pl.Buffered(1))
    return pl.pallas_call(
        _merge_kernel,
        grid=(nbg, seq // tt),
        in_specs=[tok(D_MODEL), tok(ATTN_WIDTH), slab, slab, slab, slab,
                  _const((BG, N_MOD, D_MODEL)), _const((1, D_MODEL), (l,)), _const((1, D_MODEL), (l,))]
                 + [gate(j) for j in range(N_GATE_BLK)]
                 + [wspec(ATTN_WIDTH, D_MODEL), wspec(D_RNN, D_MODEL),
                  _const((1, D_SSM), (l,)), wspec(D_SSM, 2 * D_MODEL), wspec(D_MODEL, D_MODEL),
                  wspec(D_MODEL, ROUTE_W), _const((1, ROUTE_W), (l,))],
        out_specs=[tok(D_MODEL), tok(D_MODEL), tok(ROUTE_W)],
        out_shape=[jax.ShapeDtypeStruct((batch, seq, D_MODEL), F32), jax.ShapeDtypeStruct((batch, seq, D_MODEL), BF16),
                   jax.ShapeDtypeStruct((batch, seq, ROUTE_W), F32)],
        scratch_shapes=[pltpu.VMEM((rows, D_MODEL), BF16), pltpu.VMEM((rows, D_RNN), BF16),
                        pltpu.VMEM((rows, D_SSM), BF16), pltpu.VMEM((rows, D_MODEL), F32)],
        compiler_params=_cparams(("arbitrary", "arbitrary")),
        name="mixer_merge",
    )(x, o, yr, su, yf, yb, mod, pp['norm1'], pp['norm2'], *([pp['w_in']] * N_GATE_BLK), pp['w_attn_o'],
      pp['w_rnn_o'], pp['d_skip'], pp['w_glu'], pp['w_out'], pp['w_rt'], pp['b_rt'])


def _moe_kernel(x1_ref, h2_ref, comb_ref, mod_ref, wg_ref, wu_ref, wd_ref, *rest, final):
    if final:
        gf_ref, x2_ref, y_ref, hs_scr, cs_scr, acc_scr = rest
    else:
        x2_ref, hs_scr, cs_scr, acc_scr = rest
    t = MOE_T
    comb = comb_ref[...]
    lane = lax.broadcasted_iota(jnp.int32, (t, ROUTE_W), 1)
    gid = comb[:, GID_LANE:GID_LANE + 1]
    onehot = jnp.where((lane.astype(F32) == gid) & (lane < N_EXPERT_GROUPS), 1.0, 0.0)
    r_i = lax.broadcasted_iota(jnp.int32, (t, t), 0)
    c_i = lax.broadcasted_iota(jnp.int32, (t, t), 1)
    earlier = jnp.where(c_i < r_i, 1.0, 0.0).astype(BF16)
    csum = _dot(earlier, onehot.astype(BF16))
    rank = jnp.sum(onehot * csum, axis=-1, keepdims=True)
    counts = jnp.sum(onehot, axis=0, keepdims=True)
    lane_row = lax.broadcasted_iota(jnp.int32, (1, ROUTE_W), 1)
    cnt = [jnp.sum(jnp.where(lane_row == g, counts, 0.0)).astype(jnp.int32) for g in range(N_EXPERT_GROUPS)]
    start = [jnp.int32(0)]
    for g in range(1, N_EXPERT_GROUPS):
        start.append(start[-1] + cnt[g - 1])
    start_row = jnp.zeros((1, ROUTE_W), F32)
    for g in range(1, N_EXPERT_GROUPS):
        start_row = jnp.where(lane_row == g, start[g].astype(F32), start_row)
    pos = rank + jnp.sum(onehot * start_row, axis=-1, keepdims=True)
    pt = jnp.where(c_i.astype(F32) == pos, 1.0, 0.0).astype(BF16)
    hs_scr[0:t, :] = _dot_tn(pt, h2_ref[...]).astype(BF16)
    hs_scr[t:MOE_TP, :] = jnp.zeros((MOE_TP - t, D_MODEL), BF16)
    c_hi, c_lo = _split_bf16(comb)
    cs_scr[0:t, :] = _dot_tn(pt, c_hi) + _dot_tn(pt, c_lo)
    cs_scr[t:MOE_TP, :] = jnp.zeros((MOE_TP - t, ROUTE_W), F32)
    acc_scr[...] = jnp.zeros_like(acc_scr)
    lane_w = lax.broadcasted_iota(jnp.int32, (MOE_W, ROUTE_W), 1)

    def window(g, r0):
        hs = hs_scr[pl.ds(r0, MOE_W), :]
        cw = cs_scr[pl.ds(r0, MOE_W), :]
        parts = []
        for j in range(EXPERTS_PER_GROUP):
            e = g * EXPERTS_PER_GROUP + j
            w_e = jnp.sum(jnp.where(lane_w == N_EXPERT_GROUPS + e, cw, 0.0), axis=-1, keepdims=True)
            gate = _dot(hs, wg_ref[e])
            parts.append((gate * _sigmoid(gate) * _dot(hs, wu_ref[e]) * w_e).astype(BF16))
        acc_scr[pl.ds(r0, MOE_W), :] += _dot(jnp.concatenate(parts, axis=1), wd_ref[g])

    bases = [pl.multiple_of((start[g] // ROW_ALIGN) * ROW_ALIGN, ROW_ALIGN) for g in range(N_EXPERT_GROUPS)]
    for g in range(N_EXPERT_GROUPS):
        window(g, bases[g])
    for g in range(N_EXPERT_GROUPS):
        n_win = (start[g] - bases[g] + cnt[g] + MOE_W - 1) // MOE_W

        def more(k, carry, g=g):
            window(g, pl.multiple_of(bases[g] + k * MOE_W, ROW_ALIGN))
            return carry

        lax.fori_loop(1, n_win, more, 0)
    a_hi, a_lo = _split_bf16(acc_scr[0:t, :])
    moe = _dot(pt, a_hi) + _dot(pt, a_lo)
    x2 = x1_ref[...] + moe * mod_ref[5:6, :]
    x2_ref[...] = x2
    if final:
        y_ref[...] = _rms(x2, gf_ref[...])


def _moe(x1, h2, comb, mod, pp, l, seq, g_final):
    n = x1.shape[0]
    tm = MOE_T
    final = g_final is not None
    per_batch = mod.shape[0] > 1
    assert not per_batch or seq % tm == 0
    mod_idx = (lambda i: (i * tm // seq, 0, 0)) if per_batch else (lambda i: (0, 0, 0))
    rows = lambda w: pl.BlockSpec((tm, w), lambda i: (i, 0))
    in_specs = [rows(D_MODEL), rows(D_MODEL), rows(ROUTE_W), pl.BlockSpec((None, N_MOD, D_MODEL), mod_idx),
                _const((N_EXPERTS, D_MODEL, D_EXPERT), (l,), single=True),
                _const((N_EXPERTS, D_MODEL, D_EXPERT), (l,), single=True),
                _const((N_EXPERT_GROUPS, EXPERTS_PER_GROUP * D_EXPERT, D_MODEL), (l,), single=True)]
    args = [x1, h2, comb, mod, pp['w_gate'], pp['w_up'], pp['w_down']]
    out_specs = [rows(D_MODEL)]
    out_shape = [jax.ShapeDtypeStruct((n, D_MODEL), F32)]
    if final:
        in_specs.append(pl.BlockSpec((1, D_MODEL), lambda i: (0, 0)))
        args.append(g_final)
        out_specs.append(rows(D_MODEL))
        out_shape.append(jax.ShapeDtypeStruct((n, D_MODEL), F32))
    return pl.pallas_call(
        functools.partial(_moe_kernel, final=final),
        grid=(n // tm,),
        in_specs=in_specs,
        out_specs=out_specs,
        out_shape=out_shape,
        scratch_shapes=[pltpu.VMEM((MOE_TP, D_MODEL), BF16), pltpu.VMEM((MOE_TP, ROUTE_W), F32),
                        pltpu.VMEM((MOE_TP, D_MODEL), F32)],
        compiler_params=_cparams(("arbitrary",)),
        name="hier_moe",
    )(*args)


def _head_mean_matrix(width):
    idx = np.arange(width) // HEAD_DIM
    return jnp.asarray((idx[:, None] == idx[None, :]).astype(np.float32) / HEAD_DIM, dtype=BF16)


def _rope_tables(seq):
    t = np.arange(seq)
    row = (t // GRID_W).astype(np.float64)
    col = (t % GRID_W).astype(np.float64)
    inv = ROPE_BASE ** (-np.arange(ROPE_PAIRS_PER_AXIS, dtype=np.float64) / ROPE_PAIRS_PER_AXIS)
    inv = inv.astype(np.float32).astype(np.float64)
    ang = np.concatenate([row[:, None] * inv, col[:, None] * inv], axis=-1)
    ang = ang.astype(np.float32).astype(np.float64)
    cos = np.repeat(np.cos(ang), 2, axis=-1)
    sin = np.repeat(np.sin(ang), 2, axis=-1) * np.tile(np.array([-1.0, 1.0]), HEAD_DIM // 2)
    expand = lambda a: jnp.asarray(np.tile(a, (1, LANES // HEAD_DIM)), dtype=F32)
    return expand(cos), expand(sin)


def _prep_params(p, s5p):
    bf = lambda a: a.astype(BF16)
    pp = {}
    pp['w_in'] = bf(p['w_in'])
    pp['norm1'] = p['norm1'][:, None, :]
    pp['norm2'] = p['norm2'][:, None, :]
    pp['qg'] = jnp.tile(p['q_norm'], (1, N_HEADS))[:, None, :]
    pp['kg'] = jnp.tile(p['k_norm'], (1, N_KV_HEADS))[:, None, :]
    pp['s_q'] = _head_mean_matrix(ATTN_WIDTH)
    pp['s_k'] = _head_mean_matrix(KV_WIDTH)
    pp['w_attn_o'] = bf(p['w_attn_o'])
    pp['w_rnn_o'] = bf(p['w_rnn_o'])
    pp['w_glu'] = bf(p['w_glu'])
    pp['w_out'] = bf(p['w_out'])
    pp['d_skip'] = p['ssm_d'][:, None, :]
    pp['conv_w'] = p['conv_w']
    pp['conv_b'] = p['conv_b'][:, None, :]
    pp['lam'] = p['lru_lambda']
    per = LANES // RNN_BLOCK_W
    wa, wi = p['lru_w_a'], p['lru_w_i']
    w = jnp.stack([wa[:, 0], wi[:, 0], wa[:, 1], wi[:, 1]], axis=1)
    w = w.reshape(DEPTH, 4, N_SLAB, per, RNN_BLOCK_W, RNN_BLOCK_W)
    eye = jnp.eye(per, dtype=F32)
    wg = w.transpose(0, 2, 3, 4, 1, 5)[:, :, :, :, :, None, :] * eye[None, None, :, None, None, :, None]
    pp['wg'] = bf(wg.reshape(DEPTH, N_SLAB, LANES, 4 * LANES))
    ba, bi = p['lru_b_a'], p['lru_b_i']
    b = jnp.stack([ba[:, 0], bi[:, 0], ba[:, 1], bi[:, 1]], axis=1).reshape(DEPTH, 4, N_SLAB, LANES)
    pp['bg'] = b.transpose(0, 2, 1, 3).reshape(DEPTH, N_SLAB, 1, 4 * LANES)
    ab_re, ab_im, bb_re, bb_im = s5p
    eye_g = jnp.eye(SG_GROUPS, dtype=F32)
    lead = (DEPTH, N_DIR, SSM_SG, SG_GROUPS)
    bb = jnp.stack([bb_re, bb_im], axis=2).reshape(lead + (SSM_GROUP_W, 2, SSM_STATE))
    eye_g = bf(eye_g)
    bm = bf(bb)[:, :, :, :, :, :, None, :] * eye_g[None, None, None, :, None, None, :, None]
    pp['bm'] = bm.reshape(DEPTH, N_DIR, SSM_SG, LANES, 2 * SG_STATE)
    cc = jnp.stack([p['ssm_c_re'], -p['ssm_c_im']], axis=2)
    cc = cc.reshape(DEPTH, N_DIR, 2, SSM_SG, SG_GROUPS, SSM_GROUP_W, SSM_STATE).transpose(0, 1, 3, 2, 4, 6, 5)
    cm = bf(cc)[:, :, :, :, :, :, None, :] * eye_g[None, None, None, None, :, None, :, None]
    pp['cm'] = cm.reshape(DEPTH, N_DIR, SSM_SG, 2 * SG_STATE, LANES)
    are = ab_re.reshape(DEPTH, N_DIR, SSM_SG, SG_STATE)
    aim = ab_im.reshape(DEPTH, N_DIR, SSM_SG, SG_STATE)
    a_flat = jnp.concatenate([are, aim], axis=-1).reshape(DEPTH, N_DIR, 1, SSM_FLAT)
    pp['a_flat'] = jnp.broadcast_to(a_flat, (DEPTH, N_DIR, BG, SSM_FLAT))
    fill = ROUTE_W - N_EXPERT_GROUPS - N_EXPERTS
    pp['w_rt'] = jnp.concatenate([p['router_g_w'], p['router_e_w'], jnp.zeros((DEPTH, D_MODEL, fill), F32)], axis=-1)
    pp['b_rt'] = jnp.concatenate([p['router_g_b'], p['router_e_b'], jnp.zeros((DEPTH, fill), F32)], axis=-1)[:, None, :]
    pp['w_gate'] = bf(p['w_e_gate'])
    pp['w_up'] = bf(p['w_e_up'])
    pp['w_down'] = bf(p['w_e_down']).reshape(DEPTH, N_EXPERT_GROUPS, EXPERTS_PER_GROUP * D_EXPERT, D_MODEL)
    return pp


def _flat_state(re, im):
    b = re.shape[0]
    r = re.transpose(1, 2, 0, 3, 4).reshape(DEPTH, N_DIR, b, SSM_SG, SG_STATE)
    i = im.transpose(1, 2, 0, 3, 4).reshape(DEPTH, N_DIR, b, SSM_SG, SG_STATE)
    return jnp.concatenate([r, i], axis=-1).reshape(DEPTH, N_DIR, b, SSM_FLAT)


def _unflat_state(flat):
    b = flat.shape[2]
    f = flat.reshape(DEPTH, N_DIR, b, SSM_SG, 2, SG_GROUPS, SSM_STATE).transpose(4, 2, 0, 1, 3, 5, 6)
    f = f.reshape(2, b, DEPTH, N_DIR, SSM_GROUPS, SSM_STATE)
    return f[0], f[1]


def _trunk_layer(x, mod, mod_moe, pp, l, ctx, rope_tabs, g_final, attn_tq):
    batch, seq, _ = x.shape
    q, k, v, rx, rg, su = _in_proj(x, mod, pp, l, rope_tabs)
    if ctx is None:
        attn_ctx = None
        h0 = jnp.zeros((N_DIR, batch, D_RNN), F32)
        s0 = jnp.zeros((N_DIR, batch, SSM_FLAT), F32)
    else:
        cache_k, cache_v, h0, s0 = ctx
        attn_ctx = (cache_k, cache_v, l)
    o = _attention(q, k, v, attn_ctx, attn_tq)
    yr, rnn_fin = _rglru(rx, rg, pp, l, h0)
    yf, yb, ssm_fin = _s5(su, pp, l, s0)
    x1, h2, comb = _merge(x, o, yr, su, yf, yb, mod, pp, l)
    flat = lambda a: a.reshape(batch * seq, a.shape[-1])
    outs = _moe(flat(x1), flat(h2), flat(comb), mod_moe, pp, l, seq, g_final)
    return [a.reshape(batch, seq, D_MODEL) for a in outs], (k, v, rnn_fin, ssm_fin)


def kernel(x_prompt, x_sample, cache_k, cache_v, state_rglru, state_ssm_re, state_ssm_im, c, c_ctx, w_mod, b_mod, norm1, norm2, w_in, q_norm, k_norm, w_attn_o, conv_w, conv_b, lru_w_a, lru_b_a, lru_w_i, lru_b_i, lru_lambda, w_rnn_o, ssm_lam_re, ssm_lam_im, ssm_log_step, ssm_b_re, ssm_b_im, ssm_c_re, ssm_c_im, ssm_d, w_glu, w_out, router_g_w, router_g_b, router_e_w, router_e_b, w_e_gate, w_e_up, w_e_down, final_norm):
    p = dict(norm1=norm1, norm2=norm2, w_in=w_in, q_norm=q_norm, k_norm=k_norm, w_attn_o=w_attn_o, conv_w=conv_w,
             conv_b=conv_b, lru_w_a=lru_w_a, lru_b_a=lru_b_a, lru_w_i=lru_w_i, lru_b_i=lru_b_i,
             lru_lambda=lru_lambda, w_rnn_o=w_rnn_o, ssm_c_re=ssm_c_re, ssm_c_im=ssm_c_im, ssm_d=ssm_d, w_glu=w_glu,
             w_out=w_out, router_g_w=router_g_w, router_g_b=router_g_b, router_e_w=router_e_w,
             router_e_b=router_e_b, w_e_gate=w_e_gate, w_e_up=w_e_up, w_e_down=w_e_down)
    bc, lc, _ = x_prompt.shape
    bd, ld, _ = x_sample.shape
    past = cache_k.shape[2]
    assert bd == BG and bc % BG == 0

    n_cond = 1 + bd
    cond_rows = -(-n_cond // SUBLANES) * SUBLANES
    cond = jnp.concatenate([c_ctx[None], c, jnp.zeros((cond_rows - n_cond, D_MODEL), F32)], axis=0)
    mods = _modulation(cond, w_mod, b_mod).reshape(DEPTH, cond_rows, N_MOD, D_MODEL)
    s5p = _s5_params(ssm_lam_re, ssm_lam_im, ssm_log_step, ssm_b_re, ssm_b_im)
    pp = _prep_params(p, s5p)
    rope_tabs = _rope_tables(ld)
    final_g = final_norm[None]

    ck = cache_k.reshape(bd, DEPTH, past, KV_WIDTH)
    cv = cache_v.reshape(bd, DEPTH, past, KV_WIDTH)
    h0_lat = state_rglru.transpose(1, 2, 0, 3)
    s0_lat = _flat_state(state_ssm_re, state_ssm_im)
    x_ctx, x_lat = x_prompt, x_sample
    ks, vs, rs, ss = [], [], [], []
    y_ctx = y_lat = None
    for l in range(DEPTH):
        g_final = final_g if l == DEPTH - 1 else None
        mod_ctx1 = mods[l, 0:1]
        mod_ctx = jnp.broadcast_to(mod_ctx1, (BG, N_MOD, D_MODEL))
        mod_lat = mods[l, 1:n_cond]
        outs, (k_l, v_l, r_l, s_l) = _trunk_layer(x_ctx, mod_ctx, mod_ctx1, pp, l, None, None, g_final, lc)
        x_ctx = outs[0]
        if g_final is not None:
            y_ctx = outs[1]
        ks.append(k_l)
        vs.append(v_l)
        rs.append(r_l)
        ss.append(s_l)
        ctx = (ck, cv, h0_lat[l], s0_lat[l])
        outs, _ = _trunk_layer(x_lat, mod_lat, mod_lat, pp, l, ctx, rope_tabs, g_final, 512)
        x_lat = outs[0]
        if g_final is not None:
            y_lat = outs[1]
    new_k = jnp.stack(ks, axis=1).reshape(bc, DEPTH, lc, N_KV_HEADS, HEAD_DIM)
    new_v = jnp.stack(vs, axis=1).reshape(bc, DEPTH, lc, N_KV_HEADS, HEAD_DIM)
    new_r = jnp.stack(rs, axis=0).transpose(2, 0, 1, 3)
    new_re, new_im = _unflat_state(jnp.stack(ss, axis=0))
    return (y_ctx, y_lat, new_k, new_v, new_r, new_re, new_im)
```

```python
import functools

import numpy as np
import jax
import jax.numpy as jnp
from jax import lax
from jax.experimental import pallas as pl
from jax.experimental.pallas import tpu as pltpu

F32 = jnp.float32
BF16 = jnp.bfloat16

D_MODEL = 1024
DEPTH = 2
GRID_W = 64
N_DIR = 2
NORM_EPS = 1e-6
N_MOD = 6

N_HEADS = 8
N_KV_HEADS = 4
HEAD_DIM = 64
ATTN_WIDTH = N_HEADS * HEAD_DIM
KV_WIDTH = N_KV_HEADS * HEAD_DIM
ROPE_BASE = 10000.0
ROPE_PAIRS_PER_AXIS = HEAD_DIM // 4

D_RNN = D_MODEL // 2
RNN_BLOCKS = 8
RNN_BLOCK_W = D_RNN // RNN_BLOCKS
CONV_W = 4
CONV_LEFT = 2
LRU_C = 8.0

D_SSM = D_MODEL // 2
SSM_GROUP_W = 16
SSM_GROUPS = D_SSM // SSM_GROUP_W
SSM_STATE = 64

N_EXPERT_GROUPS = 4
EXPERTS_PER_GROUP = 4
N_EXPERTS = N_EXPERT_GROUPS * EXPERTS_PER_GROUP
D_EXPERT = D_MODEL // 4

LANES = 128
SUBLANES = 8
VMEM_LIMIT = 56 * 1024 * 1024

BG = SUBLANES
PROJ_W = ATTN_WIDTH + 2 * KV_WIDTH + 2 * D_RNN + D_SSM
N_SLAB = D_RNN // LANES
GATE_BLK = 512
N_GATE_BLK = 3 * D_MODEL // GATE_BLK
assert PROJ_W % GATE_BLK == 0

SG_GROUPS = LANES // SSM_GROUP_W
SSM_SG = SSM_GROUPS // SG_GROUPS
SG_STATE = SG_GROUPS * SSM_STATE
SSM_FLAT = SSM_SG * 2 * SG_STATE

ROUTE_W = LANES
GID_LANE = ROUTE_W - 1
MOE_T = 512
MOE_W = 160
ROW_ALIGN = 16
MOE_TP = 768
assert MOE_TP >= MOE_T + MOE_W + ROW_ALIGN and MOE_TP % LANES == 0 and MOE_W % ROW_ALIGN == 0


def _cparams(sem, vmem=VMEM_LIMIT):
    return pltpu.CompilerParams(dimension_semantics=sem, vmem_limit_bytes=vmem)


def _const(shape, lead=(), single=False):
    idx = tuple(lead) + (0,) * len(shape)
    block = (None,) * len(lead) + tuple(shape)
    if single:
        return pl.BlockSpec(block, lambda *_: idx, pipeline_mode=pl.Buffered(1))
    return pl.BlockSpec(block, lambda *_: idx)


def _dot(a, b):
    return jnp.dot(a, b, preferred_element_type=F32)


def _dot_nt(a, b):
    return lax.dot_general(a, b, (((1,), (1,)), ((), ())), preferred_element_type=F32)


def _dot_tn(a, b):
    return lax.dot_general(a, b, (((0,), (0,)), ((), ())), preferred_element_type=F32)


def _split_bf16(a):
    hi = a.astype(BF16)
    lo = (a - hi.astype(F32)).astype(BF16)
    return hi, lo


def _dot3(a, b):
    a_hi, a_lo = _split_bf16(a)
    b_hi, b_lo = _split_bf16(b)
    return _dot(a_hi, b_hi) + (_dot(a_lo, b_hi) + _dot(a_hi, b_lo))


def _sigmoid(x):
    return 0.5 * jnp.tanh(0.5 * x) + 0.5


def _rms(x, g):
    ms = jnp.mean(x * x, axis=-1, keepdims=True)
    return x * lax.rsqrt(ms + NORM_EPS) * g


def _modulate(x, g, mod_ref, b, shift_i, scale_i):
    return _rms(x, g) * (1.0 + mod_ref[b, scale_i:scale_i + 1, :]) + mod_ref[b, shift_i:shift_i + 1, :]


def _strided_rows(b, n):
    return pl.ds(b, n, stride=BG)


def _mod_kernel(c_ref, w_ref, b_ref, o_ref):
    c = c_ref[...]
    a = c * _sigmoid(c)
    o_ref[...] = _dot3(a, w_ref[...]) + b_ref[...]


def _modulation(cond, w_mod, b_mod):
    rows = cond.shape[0]
    tn = 1536
    width = N_MOD * D_MODEL
    return pl.pallas_call(
        _mod_kernel,
        grid=(DEPTH, width // tn),
        in_specs=[
            pl.BlockSpec((rows, D_MODEL), lambda l, j: (0, 0)),
            pl.BlockSpec((None, D_MODEL, tn), lambda l, j: (l, 0, j)),
            pl.BlockSpec((None, 1, tn), lambda l, j: (l, 0, j)),
        ],
        out_specs=pl.BlockSpec((None, rows, tn), lambda l, j: (l, 0, j)),
        out_shape=jax.ShapeDtypeStruct((DEPTH, rows, width), F32),
        compiler_params=_cparams(("arbitrary", "arbitrary")),
        name="adaln_modulation",
    )(cond, w_mod, b_mod.reshape(DEPTH, 1, width))


def _head_norm(z, s_ref, gain):
    ms = _dot((z * z).astype(BF16), s_ref[...])
    return z * lax.rsqrt(ms + NORM_EPS) * gain


def _rope(z, cos, sin):
    rows = z.shape[0]
    tt = cos.shape[0]
    lane = lax.broadcasted_iota(jnp.int32, (rows, LANES), 1)
    partner = jnp.where((lane & 1) == 0, pltpu.roll(z, LANES - 1, 1), pltpu.roll(z, 1, 1))
    z3 = z.reshape(rows // tt, tt, LANES)
    p3 = partner.reshape(rows // tt, tt, LANES)
    return z3 * cos[None] + p3 * sin[None]


def _in_proj_kernel(x_ref, mod_ref, g_ref, w_ref, qg_ref, kg_ref, sq_ref, sk_ref, *rest, rope):
    if rope:
        cos_ref, sin_ref, q_ref, k_ref, v_ref, rx_ref, rg_ref, su_ref, h_scr = rest
    else:
        q_ref, k_ref, v_ref, rx_ref, rg_ref, su_ref, h_scr = rest
    tt = x_ref.shape[1]
    g = g_ref[...]
    qscale = HEAD_DIM ** -0.5
    for b in range(BG):
        h_scr[b * tt:(b + 1) * tt, :] = _modulate(x_ref[b], g, mod_ref, b, 0, 1).astype(BF16)
    h = h_scr[...]

    def proj(lo, width):
        return _dot(h, w_ref[:, lo:lo + width])

    q = _head_norm(proj(0, ATTN_WIDTH), sq_ref, qg_ref[...]) * qscale
    k = _head_norm(proj(ATTN_WIDTH, KV_WIDTH), sk_ref, kg_ref[...])
    if rope:
        cos = cos_ref[...]
        sin = sin_ref[...]
        for dst, z in ((q_ref, q), (k_ref, k)):
            for c in range(z.shape[1] // LANES):
                cs = slice(c * LANES, (c + 1) * LANES)
                dst[:, :, cs] = _rope(z[:, cs], cos, sin)
    else:
        q_ref[...] = q.reshape(BG, tt, ATTN_WIDTH)
        k_ref[...] = k.reshape(BG, tt, KV_WIDTH)
    off = ATTN_WIDTH + KV_WIDTH
    v_ref[...] = proj(off, KV_WIDTH).reshape(BG, tt, KV_WIDTH)
    off += KV_WIDTH
    for dst in (rx_ref, rg_ref, su_ref):
        z = proj(off, D_RNN)
        off += D_RNN
        for b in range(BG):
            for c in range(N_SLAB):
                dst.at[c][_strided_rows(b, tt), :] = z[b * tt:(b + 1) * tt, c * LANES:(c + 1) * LANES]


def _in_proj(x, mod, pp, l, rope_tabs, tt=64):
    batch, seq, _ = x.shape
    nbg = batch // BG
    rope = rope_tabs is not None
    tok = lambda w: pl.BlockSpec((BG, tt, w), lambda g, i: (g, i, 0))
    slab = pl.BlockSpec((None, N_SLAB, tt * BG, LANES), lambda g, i: (g, 0, i, 0))
    in_specs = [tok(D_MODEL), _const((BG, N_MOD, D_MODEL)), _const((1, D_MODEL), (l,)),
                pl.BlockSpec((None, D_MODEL, PROJ_W), lambda *_: (l, 0, 0), pipeline_mode=pl.Buffered(1)),
                _const((1, ATTN_WIDTH), (l,)),
                _const((1, KV_WIDTH), (l,)), _const((ATTN_WIDTH, ATTN_WIDTH)), _const((KV_WIDTH, KV_WIDTH))]
    args = [x, mod, pp['norm1'], pp['w_in'], pp['qg'], pp['kg'], pp['s_q'], pp['s_k']]
    if rope:
        tab = pl.BlockSpec((tt, LANES), lambda g, i: (i, 0))
        in_specs += [tab, tab]
        args += list(rope_tabs)
    tok_shape = lambda w: jax.ShapeDtypeStruct((batch, seq, w), F32)
    slab_shape = jax.ShapeDtypeStruct((nbg, N_SLAB, seq * BG, LANES), F32)
    return pl.pallas_call(
        functools.partial(_in_proj_kernel, rope=rope),
        grid=(nbg, seq // tt),
        in_specs=in_specs,
        out_specs=[tok(ATTN_WIDTH), tok(KV_WIDTH), tok(KV_WIDTH), slab, slab, slab],
        out_shape=[tok_shape(ATTN_WIDTH), tok_shape(KV_WIDTH), tok_shape(KV_WIDTH), slab_shape, slab_shape, slab_shape],
        scratch_shapes=[pltpu.VMEM((BG * tt, D_MODEL), BF16)],
        compiler_params=_cparams(("arbitrary", "arbitrary")),
        name="mixer_in_proj",
    )(*args)


def _attn_kernel(q_ref, k_ref, v_ref, *rest, n_ctx):
    if n_ctx:
        kc_ref, vc_ref, o_ref, kd_ref, va_ref, vb_ref = rest
    else:
        o_ref, kd_ref, va_ref, vb_ref = rest
    lk = k_ref.shape[0]
    tq = q_ref.shape[0]

    def lane_lt(rows):
        return lax.broadcasted_iota(jnp.int32, (rows, LANES), 1) < HEAD_DIM

    @pl.when(pl.program_id(1) == 0)
    def _():
        def fill(src_k, src_v, base, rows):
            low = lane_lt(rows)
            for pair in range(N_KV_HEADS // 2):
                sl = slice(pair * LANES, (pair + 1) * LANES)
                k2 = src_k[:, sl]
                v2 = src_v[:, sl]
                k2r = pltpu.roll(k2, HEAD_DIM, 1)
                v2r = pltpu.roll(v2, HEAD_DIM, 1)
                for odd in range(2):
                    j = 2 * pair + odd
                    kd = jnp.where(low, k2r, k2) if odd else jnp.where(low, k2, k2r)
                    vd = jnp.where(low, v2r, v2) if odd else jnp.where(low, v2, v2r)
                    kd_ref[j, base:base + rows, :] = kd.astype(BF16)
                    va_ref[j, base:base + rows, :] = jnp.where(low, vd, 0.0).astype(BF16)
                    vb_ref[j, base:base + rows, :] = jnp.where(low, 0.0, vd).astype(BF16)

        if n_ctx:
            fill(kc_ref, vc_ref, 0, n_ctx)
        fill(k_ref, v_ref, n_ctx, lk)

    low = lane_lt(tq)
    for j in range(N_KV_HEADS):
        q2 = q_ref[:, j * LANES:(j + 1) * LANES]
        kd = kd_ref[j]
        s0 = _dot_nt(jnp.where(low, q2, 0.0).astype(BF16), kd)
        s1 = _dot_nt(jnp.where(low, 0.0, q2).astype(BF16), kd)
        p0 = jnp.exp(s0 - jnp.max(s0, axis=-1, keepdims=True))
        p1 = jnp.exp(s1 - jnp.max(s1, axis=-1, keepdims=True))
        r0 = 1.0 / jnp.sum(p0, axis=-1, keepdims=True)
        r1 = 1.0 / jnp.sum(p1, axis=-1, keepdims=True)
        o2 = _dot(p0.astype(BF16), va_ref[j]) + _dot(p1.astype(BF16), vb_ref[j])
        o_ref[:, j * LANES:(j + 1) * LANES] = o2 * jnp.where(low, r0, r1)


def _attention(q, k, v, ctx, tq):
    batch, seq, _ = q.shape
    n_ctx = 0 if ctx is None else ctx[0].shape[2]
    tok = lambda w: pl.BlockSpec((None, seq, w), lambda b, i: (b, 0, 0))
    in_specs = [pl.BlockSpec((None, tq, ATTN_WIDTH), lambda b, i: (b, i, 0)), tok(KV_WIDTH), tok(KV_WIDTH)]
    args = [q, k, v]
    if ctx is not None:
        cache_k, cache_v, layer = ctx
        cspec = pl.BlockSpec((None, None, n_ctx, KV_WIDTH), lambda b, i: (b, layer, 0, 0))
        in_specs += [cspec, cspec]
        args += [cache_k, cache_v]
    lt = n_ctx + seq
    return pl.pallas_call(
        functools.partial(_attn_kernel, n_ctx=n_ctx),
        grid=(batch, seq // tq),
        in_specs=in_specs,
        out_specs=pl.BlockSpec((None, tq, ATTN_WIDTH), lambda b, i: (b, i, 0)),
        out_shape=jax.ShapeDtypeStruct((batch, seq, ATTN_WIDTH), F32),
        scratch_shapes=[pltpu.VMEM((N_KV_HEADS, lt, LANES), BF16)] * 3,
        compiler_params=_cparams(("arbitrary", "arbitrary")),
        name="gqa_attention",
    )(*args)


def _rglru_kernel(rx_ref, rg_ref, cw_ref, cb_ref, wg_ref, bg_ref, lam_ref, h0_ref, y_ref, fin_ref,
                  xpad_ref, af_ref, uf_ref, ab_ref, ub_ref, *, seq, chunk):
    n = seq * BG
    pad = CONV_LEFT * BG
    n_chunks = n // chunk
    zeros = jnp.zeros((pad, LANES), F32)
    xpad_ref[0:pad, :] = zeros
    xpad_ref[pad + n:pad + n + pad, :] = zeros

    def copy_in(i, c):
        r0 = pl.multiple_of(i * chunk, chunk)
        xpad_ref[pl.ds(pad + r0, chunk), :] = rx_ref[pl.ds(r0, chunk), :]
        return c

    lax.fori_loop(0, n_chunks, copy_in, 0)

    neg = -lam_ref[...]
    softplus = jnp.maximum(neg, 0.0) + jnp.log1p(jnp.exp(-jnp.abs(neg)))
    half_decay = -0.5 * LRU_C * softplus
    cw = cw_ref[...]
    cb = cb_ref[...]

    def gates(i, c):
        r0 = pl.multiple_of(i * chunk, chunk)
        xc = cb
        for j in range(CONV_W):
            xc = xc + xpad_ref[pl.ds(r0 + j * BG, chunk), :] * cw[j:j + 1, :]
        g = _dot(xc.astype(BF16), wg_ref[...]) + bg_ref[...]
        half_xc = 0.5 * xc
        for d, (a_ref, u_ref) in enumerate(((af_ref, uf_ref), (ab_ref, ub_ref))):
            hd = half_decay[d:d + 1, :]
            log_a = jnp.tanh(g[:, (2 * d) * LANES:(2 * d + 1) * LANES]) * hd + hd
            t_i = jnp.tanh(g[:, (2 * d + 1) * LANES:(2 * d + 2) * LANES])
            gated_x = t_i * half_xc + half_xc
            a_ref[pl.ds(r0, chunk), :] = jnp.exp(log_a)
            th = jnp.tanh(log_a)
            w = -2.0 * th
            mult = jnp.where(w > 0.0, w * lax.rsqrt(w * (1.0 - th)), 0.0)
            u_ref[pl.ds(r0, chunk), :] = mult * gated_x
        return c

    lax.fori_loop(0, n_chunks, gates, 0)

    def step(t, carry):
        hf, hb = carry
        rf = pl.multiple_of(t * BG, BG)
        rb = pl.multiple_of((seq - 1 - t) * BG, BG)
        hf = af_ref[pl.ds(rf, BG), :] * hf + uf_ref[pl.ds(rf, BG), :]
        y_ref[pl.ds(rf, BG), :] = hf
        hb = ab_ref[pl.ds(rb, BG), :] * hb + ub_ref[pl.ds(rb, BG), :]
        xpad_ref[pl.ds(rb, BG), :] = hb
        return hf, hb

    hf, hb = lax.fori_loop(0, seq, step, (h0_ref[0], h0_ref[1]), unroll=8)
    fin_ref[0] = hf
    fin_ref[1] = hb

    def emit(i, c):
        r0 = pl.multiple_of(i * chunk, chunk)
        gate = jax.nn.gelu(rg_ref[pl.ds(r0, chunk), :])
        y_ref[pl.ds(r0, chunk), :] = gate * (y_ref[pl.ds(r0, chunk), :] + xpad_ref[pl.ds(r0, chunk), :])
        return c

    lax.fori_loop(0, n_chunks, emit, 0)


def _rglru(rx, rg, pp, l, h0, chunk=512):
    nbg, _, n, _ = rx.shape
    seq = n // BG
    slab = pl.BlockSpec((None, None, n, LANES), lambda g, c: (g, c, 0, 0))
    col = lambda rows: pl.BlockSpec((None, rows, LANES), lambda g, c: (l, 0, c))
    state = pl.BlockSpec((N_DIR, BG, LANES), lambda g, c: (0, g, c))
    return pl.pallas_call(
        functools.partial(_rglru_kernel, seq=seq, chunk=chunk),
        grid=(nbg, N_SLAB),
        in_specs=[slab, slab, col(CONV_W), col(1),
                  pl.BlockSpec((None, None, LANES, 4 * LANES), lambda g, c: (l, c, 0, 0)),
                  pl.BlockSpec((None, None, 1, 4 * LANES), lambda g, c: (l, c, 0, 0)),
                  col(N_DIR), state],
        out_specs=[slab, state],
        out_shape=[jax.ShapeDtypeStruct(rx.shape, F32), jax.ShapeDtypeStruct((N_DIR, nbg * BG, D_RNN), F32)],
        scratch_shapes=[pltpu.VMEM((n + 2 * CONV_LEFT * BG, LANES), F32)] + [pltpu.VMEM((n, LANES), F32)] * 4,
        compiler_params=_cparams(("arbitrary", "arbitrary")),
        name="rglru_scan",
    )(rx, rg, pp['conv_w'], pp['conv_b'], pp['wg'], pp['bg'], pp['lam'], h0)


def _s5_param_kernel(lr_ref, li_ref, ls_ref, br_ref, bi_ref, are_ref, aim_ref, bbr_ref, bbi_ref):
    lr = lr_ref[...]
    li = li_ref[...]
    step = jnp.exp(ls_ref[...])
    mag = jnp.exp(lr * step)
    ab_re = mag * jnp.cos(li * step)
    ab_im = mag * jnp.sin(li * step)
    den = lr * lr + li * li
    nr = ab_re - 1.0
    ni = ab_im
    f_re = (nr * lr + ni * li) / den
    f_im = (ni * lr - nr * li) / den
    are_ref[...] = ab_re
    aim_ref[...] = ab_im
    br = br_ref[...]
    bi = bi_ref[...]
    bbr_ref[...] = f_re[:, None, :] * br - f_im[:, None, :] * bi
    bbi_ref[...] = f_re[:, None, :] * bi + f_im[:, None, :] * br


def _s5_params(lam_re, lam_im, log_step, b_re, b_im):
    rows = DEPTH * N_DIR * SSM_GROUPS
    flat = lambda a: a.reshape(rows, SSM_STATE)
    bt = lambda a: a.reshape(rows, SSM_STATE, SSM_GROUP_W).transpose(0, 2, 1)
    vec = jax.ShapeDtypeStruct((rows, SSM_STATE), F32)
    mat = jax.ShapeDtypeStruct((rows, SSM_GROUP_W, SSM_STATE), F32)
    return pl.pallas_call(
        _s5_param_kernel,
        out_shape=[vec, vec, mat, mat],
        name="s5_discretise",
    )(flat(lam_re), flat(lam_im), log_step.reshape(rows, 1), bt(b_re), bt(b_im))


def _s5_kernel(uf_ref, ub_ref, bm_ref, cm_ref, a_ref, s0_ref, yf_ref, yb_ref, fin_ref, bu_ref, xs_ref, st_ref):
    rows = uf_ref.shape[1]
    steps = rows // BG
    i = pl.program_id(1)

    @pl.when(i == 0)
    def _():
        st_ref[...] = s0_ref[...]

    for d, u_ref in enumerate((uf_ref, ub_ref)):
        for s in range(SSM_SG):
            bu_ref[d, :, s * 2 * SG_STATE:(s + 1) * 2 * SG_STATE] = _dot(u_ref[s].astype(BF16), bm_ref[d, s])

    pair_cols = [(s * 2 * SG_STATE + j * LANES, s * 2 * SG_STATE + SG_STATE + j * LANES)
                 for s in range(SSM_SG) for j in range(SG_STATE // LANES)]
    for d in range(N_DIR):
        state = [(st_ref[d, :, cr:cr + LANES], st_ref[d, :, ci:ci + LANES]) for cr, ci in pair_cols]
        for t in (range(steps) if d == 0 else range(steps - 1, -1, -1)):
            r0 = t * BG
            for p, (cr, ci) in enumerate(pair_cols):
                xr, xi = state[p]
                ar = a_ref[d, :, cr:cr + LANES]
                ai = a_ref[d, :, ci:ci + LANES]
                nr = ar * xr - ai * xi + bu_ref[d, r0:r0 + BG, cr:cr + LANES]
                ni = ar * xi + ai * xr + bu_ref[d, r0:r0 + BG, ci:ci + LANES]
                xs_ref[d, r0:r0 + BG, cr:cr + LANES] = nr
                xs_ref[d, r0:r0 + BG, ci:ci + LANES] = ni
                state[p] = (nr, ni)
        for (cr, ci), (xr, xi) in zip(pair_cols, state):
            st_ref[d, :, cr:cr + LANES] = xr
            st_ref[d, :, ci:ci + LANES] = xi

    for d, y_ref in enumerate((yf_ref, yb_ref)):
        for s in range(SSM_SG):
            xs = xs_ref[d, :, s * 2 * SG_STATE:(s + 1) * 2 * SG_STATE].astype(BF16)
            y_ref[s] = _dot(xs, cm_ref[d, s])

    @pl.when(i == pl.num_programs(1) - 1)
    def _():
        fin_ref[...] = st_ref[...]


def _s5(su, pp, l, s0, rows=512):
    nbg, _, n, _ = su.shape
    nt = n // rows
    fwd = pl.BlockSpec((None, N_SLAB, rows, LANES), lambda g, i: (g, 0, i, 0))
    bwd = pl.BlockSpec((None, N_SLAB, rows, LANES), lambda g, i: (g, 0, nt - 1 - i, 0))
    st = pl.BlockSpec((N_DIR, BG, SSM_FLAT), lambda g, i: (0, g, 0))
    return pl.pallas_call(
        _s5_kernel,
        grid=(nbg, nt),
        in_specs=[fwd, bwd, _const((N_DIR, SSM_SG, LANES, 2 * SG_STATE), (l,), single=True),
                  _const((N_DIR, SSM_SG, 2 * SG_STATE, LANES), (l,), single=True),
                  _const((N_DIR, BG, SSM_FLAT), (l,)), st],
        out_specs=[fwd, bwd, st],
        out_shape=[jax.ShapeDtypeStruct(su.shape, F32)] * 2 + [jax.ShapeDtypeStruct((N_DIR, nbg * BG, SSM_FLAT), F32)],
        scratch_shapes=[pltpu.VMEM((N_DIR, rows, SSM_FLAT), F32), pltpu.VMEM((N_DIR, rows, SSM_FLAT), F32),
                        pltpu.VMEM((N_DIR, BG, SSM_FLAT), F32)],
        compiler_params=_cparams(("arbitrary", "arbitrary")),
        name="s5_scan",
    )(su, su, pp['bm'], pp['cm'], pp['a_flat'], s0)


def _route(logits):
    rows = logits.shape[0]
    lane = lax.broadcasted_iota(jnp.int32, (rows, ROUTE_W), 1)
    neg_inf = -jnp.inf
    big = ROUTE_W

    def row_max(v):
        return jnp.max(v, axis=-1, keepdims=True)

    def first_lane(mask):
        return jnp.min(jnp.where(mask, lane, big), axis=-1, keepdims=True)

    in_g = lane < N_EXPERT_GROUPS
    gl = jnp.where(in_g, logits, neg_inf)
    g_max = row_max(gl)
    g_exp = jnp.where(in_g, jnp.exp(gl - g_max), 0.0)
    g_w = 1.0 / jnp.sum(g_exp, axis=-1, keepdims=True)
    g_sel = first_lane(gl == g_max)
    e_lo = N_EXPERT_GROUPS + g_sel * EXPERTS_PER_GROUP
    in_e = (lane >= e_lo) & (lane < e_lo + EXPERTS_PER_GROUP)
    el = jnp.where(in_e, logits, neg_inf)
    v1 = row_max(el)
    i1 = first_lane(el == v1)
    el2 = jnp.where(lane == i1, neg_inf, el)
    v2 = row_max(el2)
    i2 = first_lane(el2 == v2)
    e2 = jnp.exp(v2 - v1)
    w1 = g_w / (1.0 + e2)
    w2 = g_w * e2 / (1.0 + e2)
    comb = jnp.where(lane == i1, w1, jnp.where(lane == i2, w2, 0.0))
    return jnp.where(lane == GID_LANE, g_sel.astype(F32), comb)


def _merge_kernel(x_ref, o_ref, yr_ref, su_ref, yf_ref, yb_ref, mod_ref, g1_ref, g2_ref, *rest):
    gate_refs = rest[:N_GATE_BLK]
    (wa_ref, wr_ref, dsk_ref, wglu_ref, wout_ref, wrt_ref, brt_ref, x1_ref, h2_ref, comb_ref,
     h_scr, yr_scr, ys_scr, h2_scr) = rest[N_GATE_BLK:]
    tt = x_ref.shape[1]
    g1 = g1_ref[...]
    g2 = g2_ref[...]
    dsk = dsk_ref[...]
    per_gate = D_MODEL // GATE_BLK
    for b in range(BG):
        rb = slice(b * tt, (b + 1) * tt)
        h_scr[rb, :] = _modulate(x_ref[b], g1, mod_ref, b, 0, 1).astype(BF16)
        tm_rows = _strided_rows(b, tt)
        for c in range(N_SLAB):
            cs = slice(c * LANES, (c + 1) * LANES)
            yr_scr[rb, cs] = yr_ref.at[c][tm_rows, :].astype(BF16)
            ys = dsk[:, cs] * su_ref.at[c][tm_rows, :] + yf_ref.at[c][tm_rows, :] + yb_ref.at[c][tm_rows, :]
            ys_scr[rb, cs] = jax.nn.gelu(ys).astype(BF16)
    h = h_scr[...]
    y_attn = _dot(o_ref[...].reshape(BG * tt, ATTN_WIDTH).astype(BF16), wa_ref[...])
    y_rnn = _dot(yr_scr[...], wr_ref[...])
    glu = _dot(ys_scr[...], wglu_ref[...])
    y_ssm = glu[:, :D_MODEL] * _sigmoid(glu[:, D_MODEL:])
    mo = None
    for part in range(per_gate):
        cs = slice(part * GATE_BLK, (part + 1) * GATE_BLK)
        mix = None
        for j, y in enumerate((y_attn, y_rnn, y_ssm)):
            term = _sigmoid(_dot(h, gate_refs[j * per_gate + part][...])) * y[:, cs]
            mix = term if mix is None else mix + term
        contrib = _dot(mix.astype(BF16), wout_ref[cs, :])
        mo = contrib if mo is None else mo + contrib
    for b in range(BG):
        rb = slice(b * tt, (b + 1) * tt)
        x1 = x_ref[b] + mo[rb, :] * mod_ref[b, 2:3, :]
        x1_ref[b] = x1
        h2 = _modulate(x1, g2, mod_ref, b, 3, 4)
        h2_ref[b] = h2.astype(BF16)
        h2_scr[rb, :] = h2
    a_hi, a_lo = _split_bf16(h2_scr[...])
    w_hi, w_lo = _split_bf16(wrt_ref[...])
    both = _dot(a_hi, jnp.concatenate([w_hi, w_lo], axis=1))
    logits = both[:, :ROUTE_W] + (_dot(a_lo, w_hi) + both[:, ROUTE_W:])
    comb = _route(logits + brt_ref[...])
    comb_ref[...] = comb.reshape(BG, tt, ROUTE_W)


def _merge(x, o, yr, su, yf, yb, mod, pp, l, tt=64):
    batch, seq, _ = x.shape
    nbg = batch // BG
    rows = BG * tt
    tok = lambda w: pl.BlockSpec((BG, tt, w), lambda g, i: (g, i, 0))
    slab = pl.BlockSpec((None, N_SLAB, rows, LANES), lambda g, i: (g, 0, i, 0))
    wspec = lambda *shape: _const(shape, (l,), single=True)
    gate_blk0 = PROJ_W // GATE_BLK
    gate = lambda j: pl.BlockSpec((None, D_MODEL, GATE_BLK), lambda *_: (l, 0, gate_blk0 + j),
                                  pipeline_mode=pl.Buffered(1))
    return pl.pallas_call(
        _merge_kernel,
        grid=(nbg, seq // tt),
        in_specs=[tok(D_MODEL), tok(ATTN_WIDTH), slab, slab, slab, slab,
                  _const((BG, N_MOD, D_MODEL)), _const((1, D_MODEL), (l,)), _const((1, D_MODEL), (l,))]
                 + [gate(j) for j in range(N_GATE_BLK)]
                 + [wspec(ATTN_WIDTH, D_MODEL), wspec(D_RNN, D_MODEL),
                  _const((1, D_SSM), (l,)), wspec(D_SSM, 2 * D_MODEL), wspec(D_MODEL, D_MODEL),
                  wspec(D_MODEL, ROUTE_W), _const((1, ROUTE_W), (l,))],
        out_specs=[tok(D_MODEL), tok(D_MODEL), tok(ROUTE_W)],
        out_shape=[jax.ShapeDtypeStruct((batch, seq, D_MODEL), F32), jax.ShapeDtypeStruct((batch, seq, D_MODEL), BF16),
                   jax.ShapeDtypeStruct((batch, seq, ROUTE_W), F32)],
        scratch_shapes=[pltpu.VMEM((rows, D_MODEL), BF16), pltpu.VMEM((rows, D_RNN), BF16),
                        pltpu.VMEM((rows, D_SSM), BF16), pltpu.VMEM((rows, D_MODEL), F32)],
        compiler_params=_cparams(("arbitrary", "arbitrary")),
        name="mixer_merge",
    )(x, o, yr, su, yf, yb, mod, pp['norm1'], pp['norm2'], *([pp['w_in']] * N_GATE_BLK), pp['w_attn_o'],
      pp['w_rnn_o'], pp['d_skip'], pp['w_glu'], pp['w_out'], pp['w_rt'], pp['b_rt'])


def _moe_kernel(x1_ref, h2_ref, comb_ref, mod_ref, wg_ref, wu_ref, wd_ref, *rest, final):
    if final:
        gf_ref, x2_ref, y_ref, hs_scr, cs_scr, acc_scr = rest
    else:
        x2_ref, hs_scr, cs_scr, acc_scr = rest
    t = MOE_T
    comb = comb_ref[...]
    lane = lax.broadcasted_iota(jnp.int32, (t, ROUTE_W), 1)
    gid = comb[:, GID_LANE:GID_LANE + 1]
    onehot = jnp.where((lane.astype(F32) == gid) & (lane < N_EXPERT_GROUPS), 1.0, 0.0)
    r_i = lax.broadcasted_iota(jnp.int32, (t, t), 0)
    c_i = lax.broadcasted_iota(jnp.int32, (t, t), 1)
    earlier = jnp.where(c_i < r_i, 1.0, 0.0).astype(BF16)
    csum = _dot(earlier, onehot.astype(BF16))
    rank = jnp.sum(onehot * csum, axis=-1, keepdims=True)
    counts = jnp.sum(onehot, axis=0, keepdims=True)
    lane_row = lax.broadcasted_iota(jnp.int32, (1, ROUTE_W), 1)
    cnt = [jnp.sum(jnp.where(lane_row == g, counts, 0.0)).astype(jnp.int32) for g in range(N_EXPERT_GROUPS)]
    start = [jnp.int32(0)]
    for g in range(1, N_EXPERT_GROUPS):
        start.append(start[-1] + cnt[g - 1])
    start_row = jnp.zeros((1, ROUTE_W), F32)
    for g in range(1, N_EXPERT_GROUPS):
        start_row = jnp.where(lane_row == g, start[g].astype(F32), start_row)
    pos = rank + jnp.sum(onehot * start_row, axis=-1, keepdims=True)
    pt = jnp.where(c_i.astype(F32) == pos, 1.0, 0.0).astype(BF16)
    hs_scr[0:t, :] = _dot_tn(pt, h2_ref[...]).astype(BF16)
    hs_scr[t:MOE_TP, :] = jnp.zeros((MOE_TP - t, D_MODEL), BF16)
    c_hi, c_lo = _split_bf16(comb)
    cs_scr[0:t, :] = _dot_tn(pt, c_hi) + _dot_tn(pt, c_lo)
    cs_scr[t:MOE_TP, :] = jnp.zeros((MOE_TP - t, ROUTE_W), F32)
    acc_scr[...] = jnp.zeros_like(acc_scr)
    lane_w = lax.broadcasted_iota(jnp.int32, (MOE_W, ROUTE_W), 1)

    def window(g, r0):
        hs = hs_scr[pl.ds(r0, MOE_W), :]
        cw = cs_scr[pl.ds(r0, MOE_W), :]
        parts = []
        for j in range(EXPERTS_PER_GROUP):
            e = g * EXPERTS_PER_GROUP + j
            w_e = jnp.sum(jnp.where(lane_w == N_EXPERT_GROUPS + e, cw, 0.0), axis=-1, keepdims=True)
            gate = _dot(hs, wg_ref[e])
            parts.append((gate * _sigmoid(gate) * _dot(hs, wu_ref[e]) * w_e).astype(BF16))
        acc_scr[pl.ds(r0, MOE_W), :] += _dot(jnp.concatenate(parts, axis=1), wd_ref[g])

    bases = [pl.multiple_of((start[g] // ROW_ALIGN) * ROW_ALIGN, ROW_ALIGN) for g in range(N_EXPERT_GROUPS)]
    for g in range(N_EXPERT_GROUPS):
        window(g, bases[g])
    for g in range(N_EXPERT_GROUPS):
        n_win = (start[g] - bases[g] + cnt[g] + MOE_W - 1) // MOE_W

        def more(k, carry, g=g):
            window(g, pl.multiple_of(bases[g] + k * MOE_W, ROW_ALIGN))
            return carry

        lax.fori_loop(1, n_win, more, 0)
    a_hi, a_lo = _split_bf16(acc_scr[0:t, :])
    moe = _dot(pt, a_hi) + _dot(pt, a_lo)
    x2 = x1_ref[...] + moe * mod_ref[5:6, :]
    x2_ref[...] = x2
    if final:
        y_ref[...] = _rms(x2, gf_ref[...])


def _moe(x1, h2, comb, mod, pp, l, seq, g_final):
    n = x1.shape[0]
    tm = MOE_T
    final = g_final is not None
    per_batch = mod.shape[0] > 1
    assert not per_batch or seq % tm == 0
    mod_idx = (lambda i: (i * tm // seq, 0, 0)) if per_batch else (lambda i: (0, 0, 0))
    rows = lambda w: pl.BlockSpec((tm, w), lambda i: (i, 0))
    in_specs = [rows(D_MODEL), rows(D_MODEL), rows(ROUTE_W), pl.BlockSpec((None, N_MOD, D_MODEL), mod_idx),
                _const((N_EXPERTS, D_MODEL, D_EXPERT), (l,), single=True),
                _const((N_EXPERTS, D_MODEL, D_EXPERT), (l,), single=True),
                _const((N_EXPERT_GROUPS, EXPERTS_PER_GROUP * D_EXPERT, D_MODEL), (l,), single=True)]
    args = [x1, h2, comb, mod, pp['w_gate'], pp['w_up'], pp['w_down']]
    out_specs = [rows(D_MODEL)]
    out_shape = [jax.ShapeDtypeStruct((n, D_MODEL), F32)]
    if final:
        in_specs.append(pl.BlockSpec((1, D_MODEL), lambda i: (0, 0)))
        args.append(g_final)
        out_specs.append(rows(D_MODEL))
        out_shape.append(jax.ShapeDtypeStruct((n, D_MODEL), F32))
    return pl.pallas_call(
        functools.partial(_moe_kernel, final=final),
        grid=(n // tm,),
        in_specs=in_specs,
        out_specs=out_specs,
        out_shape=out_shape,
        scratch_shapes=[pltpu.VMEM((MOE_TP, D_MODEL), BF16), pltpu.VMEM((MOE_TP, ROUTE_W), F32),
                        pltpu.VMEM((MOE_TP, D_MODEL), F32)],
        compiler_params=_cparams(("arbitrary",)),
        name="hier_moe",
    )(*args)


def _head_mean_matrix(width):
    idx = np.arange(width) // HEAD_DIM
    return jnp.asarray((idx[:, None] == idx[None, :]).astype(np.float32) / HEAD_DIM, dtype=BF16)


def _rope_tables(seq):
    t = np.arange(seq)
    row = (t // GRID_W).astype(np.float64)
    col = (t % GRID_W).astype(np.float64)
    inv = ROPE_BASE ** (-np.arange(ROPE_PAIRS_PER_AXIS, dtype=np.float64) / ROPE_PAIRS_PER_AXIS)
    inv = inv.astype(np.float32).astype(np.float64)
    ang = np.concatenate([row[:, None] * inv, col[:, None] * inv], axis=-1)
    ang = ang.astype(np.float32).astype(np.float64)
    cos = np.repeat(np.cos(ang), 2, axis=-1)
    sin = np.repeat(np.sin(ang), 2, axis=-1) * np.tile(np.array([-1.0, 1.0]), HEAD_DIM // 2)
    expand = lambda a: jnp.asarray(np.tile(a, (1, LANES // HEAD_DIM)), dtype=F32)
    return expand(cos), expand(sin)


def _prep_params(p, s5p):
    bf = lambda a: a.astype(BF16)
    pp = {}
    pp['w_in'] = bf(p['w_in'])
    pp['norm1'] = p['norm1'][:, None, :]
    pp['norm2'] = p['norm2'][:, None, :]
    pp['qg'] = jnp.tile(p['q_norm'], (1, N_HEADS))[:, None, :]
    pp['kg'] = jnp.tile(p['k_norm'], (1, N_KV_HEADS))[:, None, :]
    pp['s_q'] = _head_mean_matrix(ATTN_WIDTH)
    pp['s_k'] = _head_mean_matrix(KV_WIDTH)
    pp['w_attn_o'] = bf(p['w_attn_o'])
    pp['w_rnn_o'] = bf(p['w_rnn_o'])
    pp['w_glu'] = bf(p['w_glu'])
    pp['w_out'] = bf(p['w_out'])
    pp['d_skip'] = p['ssm_d'][:, None, :]
    pp['conv_w'] = p['conv_w']
    pp['conv_b'] = p['conv_b'][:, None, :]
    pp['lam'] = p['lru_lambda']
    per = LANES // RNN_BLOCK_W
    wa, wi = p['lru_w_a'], p['lru_w_i']
    w = jnp.stack([wa[:, 0], wi[:, 0], wa[:, 1], wi[:, 1]], axis=1)
    w = w.reshape(DEPTH, 4, N_SLAB, per, RNN_BLOCK_W, RNN_BLOCK_W)
    eye = jnp.eye(per, dtype=F32)
    wg = w.transpose(0, 2, 3, 4, 1, 5)[:, :, :, :, :, None, :] * eye[None, None, :, None, None, :, None]
    pp['wg'] = bf(0.5 * wg.reshape(DEPTH, N_SLAB, LANES, 4 * LANES))
    ba, bi = p['lru_b_a'], p['lru_b_i']
    b = jnp.stack([ba[:, 0], bi[:, 0], ba[:, 1], bi[:, 1]], axis=1).reshape(DEPTH, 4, N_SLAB, LANES)
    pp['bg'] = 0.5 * b.transpose(0, 2, 1, 3).reshape(DEPTH, N_SLAB, 1, 4 * LANES)
    ab_re, ab_im, bb_re, bb_im = s5p
    eye_g = jnp.eye(SG_GROUPS, dtype=F32)
    lead = (DEPTH, N_DIR, SSM_SG, SG_GROUPS)
    bb = jnp.stack([bb_re, bb_im], axis=2).reshape(lead + (SSM_GROUP_W, 2, SSM_STATE))
    eye_g = bf(eye_g)
    bm = bf(bb)[:, :, :, :, :, :, None, :] * eye_g[None, None, None, :, None, None, :, None]
    pp['bm'] = bm.reshape(DEPTH, N_DIR, SSM_SG, LANES, 2 * SG_STATE)
    cc = jnp.stack([p['ssm_c_re'], -p['ssm_c_im']], axis=2)
    cc = cc.reshape(DEPTH, N_DIR, 2, SSM_SG, SG_GROUPS, SSM_GROUP_W, SSM_STATE).transpose(0, 1, 3, 2, 4, 6, 5)
    cm = bf(cc)[:, :, :, :, :, :, None, :] * eye_g[None, None, None, None, :, None, :, None]
    pp['cm'] = cm.reshape(DEPTH, N_DIR, SSM_SG, 2 * SG_STATE, LANES)
    are = ab_re.reshape(DEPTH, N_DIR, SSM_SG, SG_STATE)
    aim = ab_im.reshape(DEPTH, N_DIR, SSM_SG, SG_STATE)
    a_flat = jnp.concatenate([are, aim], axis=-1).reshape(DEPTH, N_DIR, 1, SSM_FLAT)
    pp['a_flat'] = jnp.broadcast_to(a_flat, (DEPTH, N_DIR, BG, SSM_FLAT))
    fill = ROUTE_W - N_EXPERT_GROUPS - N_EXPERTS
    pp['w_rt'] = jnp.concatenate([p['router_g_w'], p['router_e_w'], jnp.zeros((DEPTH, D_MODEL, fill), F32)], axis=-1)
    pp['b_rt'] = jnp.concatenate([p['router_g_b'], p['router_e_b'], jnp.zeros((DEPTH, fill), F32)], axis=-1)[:, None, :]
    pp['w_gate'] = bf(p['w_e_gate'])
    pp['w_up'] = bf(p['w_e_up'])
    pp['w_down'] = bf(p['w_e_down']).reshape(DEPTH, N_EXPERT_GROUPS, EXPERTS_PER_GROUP * D_EXPERT, D_MODEL)
    return pp


def _flat_state(re, im):
    b = re.shape[0]
    r = re.transpose(1, 2, 0, 3, 4).reshape(DEPTH, N_DIR, b, SSM_SG, SG_STATE)
    i = im.transpose(1, 2, 0, 3, 4).reshape(DEPTH, N_DIR, b, SSM_SG, SG_STATE)
    return jnp.concatenate([r, i], axis=-1).reshape(DEPTH, N_DIR, b, SSM_FLAT)


def _unflat_state(flat):
    b = flat.shape[2]
    f = flat.reshape(DEPTH, N_DIR, b, SSM_SG, 2, SG_GROUPS, SSM_STATE).transpose(4, 2, 0, 1, 3, 5, 6)
    f = f.reshape(2, b, DEPTH, N_DIR, SSM_GROUPS, SSM_STATE)
    return f[0], f[1]


def _trunk_layer(x, mod, mod_moe, pp, l, ctx, rope_tabs, g_final, attn_tq):
    batch, seq, _ = x.shape
    q, k, v, rx, rg, su = _in_proj(x, mod, pp, l, rope_tabs)
    if ctx is None:
        attn_ctx = None
        h0 = jnp.zeros((N_DIR, batch, D_RNN), F32)
        s0 = jnp.zeros((N_DIR, batch, SSM_FLAT), F32)
    else:
        cache_k, cache_v, h0, s0 = ctx
        attn_ctx = (cache_k, cache_v, l)
    o = _attention(q, k, v, attn_ctx, attn_tq)
    yr, rnn_fin = _rglru(rx, rg, pp, l, h0)
    yf, yb, ssm_fin = _s5(su, pp, l, s0)
    x1, h2, comb = _merge(x, o, yr, su, yf, yb, mod, pp, l)
    flat = lambda a: a.reshape(batch * seq, a.shape[-1])
    outs = _moe(flat(x1), flat(h2), flat(comb), mod_moe, pp, l, seq, g_final)
    return [a.reshape(batch, seq, D_MODEL) for a in outs], (k, v, rnn_fin, ssm_fin)


def kernel(x_prompt, x_sample, cache_k, cache_v, state_rglru, state_ssm_re, state_ssm_im, c, c_ctx, w_mod, b_mod, norm1, norm2, w_in, q_norm, k_norm, w_attn_o, conv_w, conv_b, lru_w_a, lru_b_a, lru_w_i, lru_b_i, lru_lambda, w_rnn_o, ssm_lam_re, ssm_lam_im, ssm_log_step, ssm_b_re, ssm_b_im, ssm_c_re, ssm_c_im, ssm_d, w_glu, w_out, router_g_w, router_g_b, router_e_w, router_e_b, w_e_gate, w_e_up, w_e_down, final_norm):
    p = dict(norm1=norm1, norm2=norm2, w_in=w_in, q_norm=q_norm, k_norm=k_norm, w_attn_o=w_attn_o, conv_w=conv_w,
             conv_b=conv_b, lru_w_a=lru_w_a, lru_b_a=lru_b_a, lru_w_i=lru_w_i, lru_b_i=lru_b_i,
             lru_lambda=lru_lambda, w_rnn_o=w_rnn_o, ssm_c_re=ssm_c_re, ssm_c_im=ssm_c_im, ssm_d=ssm_d, w_glu=w_glu,
             w_out=w_out, router_g_w=router_g_w, router_g_b=router_g_b, router_e_w=router_e_w,
             router_e_b=router_e_b, w_e_gate=w_e_gate, w_e_up=w_e_up, w_e_down=w_e_down)
    bc, lc, _ = x_prompt.shape
    bd, ld, _ = x_sample.shape
    past = cache_k.shape[2]
    assert bd == BG and bc % BG == 0

    n_cond = 1 + bd
    cond_rows = -(-n_cond // SUBLANES) * SUBLANES
    cond = jnp.concatenate([c_ctx[None], c, jnp.zeros((cond_rows - n_cond, D_MODEL), F32)], axis=0)
    mods = _modulation(cond, w_mod, b_mod).reshape(DEPTH, cond_rows, N_MOD, D_MODEL)
    s5p = _s5_params(ssm_lam_re, ssm_lam_im, ssm_log_step, ssm_b_re, ssm_b_im)
    pp = _prep_params(p, s5p)
    rope_tabs = _rope_tables(ld)
    final_g = final_norm[None]

    ck = cache_k.reshape(bd, DEPTH, past, KV_WIDTH)
    cv = cache_v.reshape(bd, DEPTH, past, KV_WIDTH)
    h0_lat = state_rglru.transpose(1, 2, 0, 3)
    s0_lat = _flat_state(state_ssm_re, state_ssm_im)
    x_ctx, x_lat = x_prompt, x_sample
    ks, vs, rs, ss = [], [], [], []
    y_ctx = y_lat = None
    for l in range(DEPTH):
        g_final = final_g if l == DEPTH - 1 else None
        mod_ctx1 = mods[l, 0:1]
        mod_ctx = jnp.broadcast_to(mod_ctx1, (BG, N_MOD, D_MODEL))
        mod_lat = mods[l, 1:n_cond]
        outs, (k_l, v_l, r_l, s_l) = _trunk_layer(x_ctx, mod_ctx, mod_ctx1, pp, l, None, None, g_final, lc)
        x_ctx = outs[0]
        if g_final is not None:
            y_ctx = outs[1]
        ks.append(k_l)
        vs.append(v_l)
        rs.append(r_l)
        ss.append(s_l)
        ctx = (ck, cv, h0_lat[l], s0_lat[l])
        outs, _ = _trunk_layer(x_lat, mod_lat, mod_lat, pp, l, ctx, rope_tabs, g_final, 512)
        x_lat = outs[0]
        if g_final is not None:
            y_lat = outs[1]
    new_k = jnp.stack(ks, axis=1).reshape(bc, DEPTH, lc, N_KV_HEADS, HEAD_DIM)
    new_v = jnp.stack(vs, axis=1).reshape(bc, DEPTH, lc, N_KV_HEADS, HEAD_DIM)
    new_r = jnp.stack(rs, axis=0).transpose(2, 0, 1, 3)
    new_re, new_im = _unflat_state(jnp.stack(ss, axis=0))
    return (y_ctx, y_lat, new_k, new_v, new_r, new_re, new_im)
```
